```python
import math
import jax
import jax.numpy as jnp
from jax import lax
import numpy as np

D_MODEL = 1024
BATCH = 8
SEQ = 2048
DEPTH = 4

GRID_W = 64
CTX_LEN = 256
N_MIXERS = 3

DN_ALPHA = (2 * DEPTH) ** 0.25
DN_BETA = (8 * DEPTH) ** -0.25
LN_EPS = 1e-5
FFN_RES_W = 0.5

N_MOD = 9

D_FF = 2816

DA_HEADS = 8
DA_HEAD_DIM = D_MODEL // DA_HEADS // 2
DA_SCALE = DA_HEAD_DIM ** -0.5
Q_BLOCK = 128
ROPE_BASE = 10000.0
ROPE_AXIS_DIM = DA_HEAD_DIM // 2
ROPE_PAIRS = ROPE_AXIS_DIM // 2

HY_SHORT = 3
HY_EMB = 33
HY_BANDS = (HY_EMB - 1) // 2
HY_FILTER_W = 64
HY_TARGET = 1e-2
HY_MIN_DECAY = math.log(HY_TARGET) / 0.3
HY_MAX_DECAY = math.log(HY_TARGET) / 1.5
HY_SHIFT = 0.05

SSM_D_INNER = 2 * D_MODEL
SSM_HEAD_DIM = 64
SSM_HEADS = SSM_D_INNER // SSM_HEAD_DIM
SSM_GROUPS = 4
SSM_STATE = 128
SSM_CONV = 3
SSM_CHUNK = 128
SSM_GN = SSM_GROUPS * SSM_STATE
SSM_CONV_DIM = SSM_D_INNER + 2 * SSM_GN
SSM_IN_DIM = SSM_D_INNER + SSM_CONV_DIM + 2 * SSM_HEADS

N_ATTN = (DEPTH + 2) // 3
N_HYENA = (DEPTH + 1) // 3
N_SSM = DEPTH // 3

kernel_name = "hybrid_diffattn_hyena_ssd_prefix_dit"


def layer_norm(x, g, b):
    xf = x.astype(jnp.float32)
    mu = jnp.mean(xf, -1, keepdims=True)
    var = jnp.mean(jnp.square(xf - mu), -1, keepdims=True)
    return ((xf - mu) * lax.rsqrt(var + LN_EPS)).astype(x.dtype) * g + b


def rms_norm(x, g):
    xf = x.astype(jnp.float32)
    return (xf * lax.rsqrt(jnp.mean(xf * xf, -1, keepdims=True) + LN_EPS)).astype(x.dtype) * g


def modulate(x, shift, scale):
    return x * (1.0 + scale) + shift


def post_norm(x, y, gate, res_w, g, b):
    return layer_norm(DN_ALPHA * x + res_w * (1.0 + gate) * y, g, b)


def swiglu(h, w13, w2):
    a, u = jnp.split(h @ w13, 2, axis=-1)
    return (jax.nn.silu(a) * u) @ w2


def dwconv_centred(u, w, b):
    k = w.shape[0]
    pad = k // 2
    L = u.shape[1]
    up = jnp.pad(u, ((0, 0), (pad, pad), (0, 0)))
    out = b + up[:, 0:L] * w[0]
    for t in range(1, k):
        out = out + up[:, t:t + L] * w[t]
    return out


def axial_rope(L):
    rows = L // GRID_W
    row = jnp.broadcast_to(jnp.arange(rows, dtype=jnp.float32)[:, None], (rows, GRID_W)).reshape(L)
    col = jnp.broadcast_to(jnp.arange(GRID_W, dtype=jnp.float32)[None, :], (rows, GRID_W)).reshape(L)
    inv = ROPE_BASE ** (-jnp.arange(ROPE_PAIRS, dtype=jnp.float32) / ROPE_PAIRS)
    ang = jnp.stack([row[:, None] * inv, col[:, None] * inv], axis=1)
    ang = jnp.concatenate([ang, ang], -1)
    return jnp.cos(ang), jnp.sin(ang)


def apply_rope(x, cos, sin):
    sh = x.shape
    xs = x.reshape(sh[:-1] + (2, ROPE_AXIS_DIM))
    x1, x2 = jnp.split(xs, 2, axis=-1)
    rot = jnp.concatenate([-x2, x1], -1)
    c = cos[None, :, None, None].astype(x.dtype)
    s = sin[None, :, None, None].astype(x.dtype)
    return (xs * c + rot * s).reshape(sh)


def diff_qkv(h, w_qkv):
    b, L = h.shape[0], h.shape[1]
    q, k, v = jnp.split(h @ w_qkv, 3, axis=-1)
    return (q.reshape(b, L, DA_HEADS, 2, DA_HEAD_DIM),
            k.reshape(b, L, DA_HEADS, 2, DA_HEAD_DIM),
            v.reshape(b, L, DA_HEADS, 2 * DA_HEAD_DIM))


def diff_attend(q, k, v, lam_full):
    s = jnp.einsum('bqhtd,bkhtd->bhtqk', q, k).astype(jnp.float32) * DA_SCALE
    p = jax.nn.softmax(s, axis=-1)
    a = p[:, :, 0] - lam_full * p[:, :, 1]
    return jnp.einsum('bhqk,bkhe->bqhe', a.astype(v.dtype), v)


def diff_out(o, subln_g, lam_init, w_o):
    b, L = o.shape[0], o.shape[1]
    o = rms_norm(o, subln_g) * (1.0 - lam_init)
    return o.reshape(b, L, D_MODEL) @ w_o


def diff_attention(hc, hl, w_qkv, w_o, lam, subln_g, layer_idx, ctx_out):
    b, L = hl.shape[0], hl.shape[1]
    lam_init = 0.8 - 0.6 * math.exp(-0.3 * layer_idx)
    lf = lam.astype(jnp.float32)
    lam_full = jnp.exp(jnp.sum(lf[0] * lf[1])) - jnp.exp(jnp.sum(lf[2] * lf[3])) + lam_init
    qc, kc, vc = diff_qkv(hc, w_qkv)
    ql, kl, vl = diff_qkv(hl, w_qkv)
    cos, sin = axial_rope(L)
    ql = apply_rope(ql, cos, sin)
    kl = apply_rope(kl, cos, sin)
    k_all = jnp.concatenate([kc, kl], axis=1)
    v_all = jnp.concatenate([vc, vl], axis=1)
    nb = L // Q_BLOCK
    qb = ql.reshape(b, nb, Q_BLOCK, DA_HEADS, 2, DA_HEAD_DIM).transpose(1, 0, 2, 3, 4, 5)
    ob = lax.map(lambda q: diff_attend(q, k_all, v_all, lam_full), qb)
    ol = ob.transpose(1, 0, 2, 3, 4).reshape(b, L, DA_HEADS, 2 * DA_HEAD_DIM)
    yl = diff_out(ol, subln_g, lam_init, w_o)
    yc = diff_out(diff_attend(qc, kc, vc, lam_full), subln_g, lam_init, w_o) if ctx_out else None
    return yc, yl


def hyena_filter(L, f_w_in, f_w_mid, f_b, f_freq, f_w_out):
    f32 = jnp.float32
    t = jnp.linspace(0.0, 1.0, L, dtype=f32)[:, None]
    w = 2.0 * math.pi * jnp.arange(L, dtype=f32)[:, None] / L
    f = jnp.linspace(1e-4, HY_BANDS - 1, HY_BANDS, dtype=f32)
    z = jnp.concatenate([t, jnp.cos(f * w), -jnp.sin(f * w)], axis=-1)
    freq = f_freq.astype(f32)
    h = jnp.sin(freq * (z @ f_w_in.astype(f32) + f_b[0].astype(f32)))
    h = jnp.sin(freq * (h @ f_w_mid[0].astype(f32) + f_b[1].astype(f32)))
    h = jnp.sin(freq * (h @ f_w_mid[1].astype(f32) + f_b[2].astype(f32)))
    h = (h @ f_w_out.astype(f32)).reshape(L, 2, D_MODEL)
    deltas = jnp.abs(jnp.linspace(HY_MIN_DECAY, HY_MAX_DECAY, D_MODEL, dtype=f32))
    window = jnp.exp(-t * deltas) + HY_SHIFT
    h = h * window[:, None, :]
    h_fwd, h_bwd = h[:, 0], h[:, 1]
    return jnp.concatenate([h_fwd, jnp.zeros((1, D_MODEL), f32), h_bwd[:0:-1]], axis=0)


def long_conv_bidir(u, k_full, bias):
    L = u.shape[1]
    uf32 = u.astype(jnp.float32)
    uf = jnp.fft.rfft(uf32, n=2 * L, axis=1)
    kf = jnp.fft.rfft(k_full, n=2 * L, axis=0)
    y = jnp.fft.irfft(uf * kf[None], n=2 * L, axis=1)[:, :L]
    return (y + uf32 * bias.astype(jnp.float32)).astype(u.dtype)


def hyena_mixer(h, w_in, b_in, conv_w, conv_b, f_w_in, f_w_mid, f_b, f_freq, f_w_out, bias, w_out, b_out):
    L = h.shape[1]
    k_full = hyena_filter(L, f_w_in, f_w_mid, f_b, f_freq, f_w_out)
    u = dwconv_centred(h @ w_in + b_in, conv_w, conv_b)
    x0, x1, v = jnp.split(u, 3, axis=-1)
    v = long_conv_bidir(v * x1, k_full, bias)
    return (v * x0) @ w_out + b_out


def segsum(a):
    T = a.shape[-1]
    cs = jnp.cumsum(a, axis=-1)
    d = cs[..., :, None] - cs[..., None, :]
    mask = jnp.tril(jnp.ones((T, T), dtype=bool))
    return jnp.where(mask, d, -jnp.inf)


def ssd_scan(xdt, adt, bm, cm, init):
    b, L, h, p = xdt.shape
    g, n = bm.shape[2], bm.shape[3]
    r = h // g
    nc = L // SSM_CHUNK
    X = xdt.reshape(b, nc, SSM_CHUNK, g, r, p)
    A = adt.reshape(b, nc, SSM_CHUNK, g, r).transpose(0, 3, 4, 1, 2)
    Bc = bm.reshape(b, nc, SSM_CHUNK, g, n)
    Cc = cm.reshape(b, nc, SSM_CHUNK, g, n)
    A_cs = jnp.cumsum(A, axis=-1)
    Lmat = jnp.exp(segsum(A))
    CB = jnp.einsum('bclgn,bcsgn->bgcls', Cc, Bc)
    y_diag = jnp.einsum('bgrcls,bcsgrp->bclgrp', CB[:, :, None] * Lmat, X)
    decay_states = jnp.exp(A_cs[..., -1:] - A_cs).transpose(0, 3, 4, 1, 2)
    states = jnp.einsum('bclgn,bclgrp->bcgrpn', Bc, X * decay_states[..., None])
    states = jnp.concatenate([init.reshape(b, 1, g, r, p, n), states], axis=1)
    chunk_decay = jnp.exp(segsum(jnp.pad(A_cs[..., -1], ((0, 0), (0, 0), (0, 0), (1, 0)))))
    states = jnp.einsum('bgrzc,bcgrpn->bzgrpn', chunk_decay, states)
    prev, final = states[:, :-1], states[:, -1]
    y_off = jnp.einsum('bclgn,bcgrpn->bclgrp', Cc, prev) * jnp.exp(A_cs).transpose(0, 3, 4, 1, 2)[..., None]
    return (y_diag + y_off).reshape(b, L, h, p), final.reshape(b, h, p, n)


def ssm_scan(h, w_in, conv_w, conv_b, dt_bias, a_log, d_skip, init_f, init_b):
    f32 = jnp.float32
    b, L = h.shape[0], h.shape[1]
    zxbcdt = h @ w_in
    z = zxbcdt[..., :SSM_D_INNER]
    xbc = zxbcdt[..., SSM_D_INNER:SSM_D_INNER + SSM_CONV_DIM]
    dt = zxbcdt[..., SSM_D_INNER + SSM_CONV_DIM:].reshape(b, L, 2, SSM_HEADS)
    xbc = jax.nn.silu(dwconv_centred(xbc, conv_w, conv_b)).astype(f32)
    xs = xbc[..., :SSM_D_INNER].reshape(b, L, SSM_HEADS, SSM_HEAD_DIM)
    bm = xbc[..., SSM_D_INNER:SSM_D_INNER + SSM_GN].reshape(b, L, SSM_GROUPS, SSM_STATE)
    cm = xbc[..., SSM_D_INNER + SSM_GN:].reshape(b, L, SSM_GROUPS, SSM_STATE)
    dt = jax.nn.softplus(dt.astype(f32) + dt_bias.astype(f32))
    a = -jnp.exp(a_log.astype(f32))
    y_f, s_f = ssd_scan(xs * dt[:, :, 0, :, None], a[0] * dt[:, :, 0], bm, cm, init_f)
    y_b, s_b = ssd_scan(jnp.flip(xs * dt[:, :, 1, :, None], 1), jnp.flip(a[1] * dt[:, :, 1], 1),
                        jnp.flip(bm, 1), jnp.flip(cm, 1), init_b)
    y = y_f + jnp.flip(y_b, 1) + xs * d_skip.astype(f32)[:, None]
    return y.reshape(b, L, SSM_D_INNER), z, s_f, s_b


def ssm_out(y, z, norm_g, w_out):
    b, L = y.shape[0], y.shape[1]
    yg = (y * jax.nn.silu(z.astype(jnp.float32))).reshape(b, L, SSM_GROUPS, SSM_D_INNER // SSM_GROUPS)
    yg = yg * lax.rsqrt(jnp.mean(yg * yg, -1, keepdims=True) + LN_EPS)
    return (yg.reshape(b, L, SSM_D_INNER).astype(z.dtype) * norm_g) @ w_out


def ssm_mixer(hc, hl, w_in, conv_w, conv_b, dt_bias, a_log, d_skip, norm_g, w_out, ctx_out):
    zero = jnp.zeros((hl.shape[0], SSM_HEADS, SSM_HEAD_DIM, SSM_STATE), jnp.float32)
    yc_in, zc, s_f, s_b = ssm_scan(hc, w_in, conv_w, conv_b, dt_bias, a_log, d_skip, zero, zero)
    yl_in, zl, _, _ = ssm_scan(hl, w_in, conv_w, conv_b, dt_bias, a_log, d_skip, s_f, s_b)
    yl = ssm_out(yl_in, zl, norm_g, w_out)
    yc = ssm_out(yc_in, zc, norm_g, w_out) if ctx_out else None
    return yc, yl


def setup_inputs(seed: int = 0) -> dict:
    key = jax.random.key(seed)
    ks = iter(jax.random.split(key, 48))
    f32 = jnp.float32

    def nrm(shape, scale):
        return jax.random.normal(next(ks), shape, f32) * scale

    D, F = D_MODEL, D_FF
    u_dt = jax.random.uniform(next(ks), (N_SSM, 2, SSM_HEADS), f32)
    dt0 = jnp.exp(u_dt * (math.log(0.1) - math.log(1e-3)) + math.log(1e-3))
    return {
        "x": nrm((BATCH, SEQ, D), 1.0),
        "c": nrm((BATCH, D), 1.0),
        "ctx": nrm((BATCH, CTX_LEN, D), 1.0),
        "c_ctx": nrm((D,), 1.0),
        "ada_w": nrm((DEPTH, D, N_MOD * D), 0.1 * D ** -0.5),
        "ada_b": nrm((DEPTH, N_MOD * D), 0.01),
        "ln_g": 1.0 + nrm((DEPTH, 3, D), 0.1),
        "ln_b": nrm((DEPTH, 3, D), 0.01),
        "ffn_w13": nrm((DEPTH, 2, D, 2 * F), D ** -0.5),
        "ffn_w2": nrm((DEPTH, 2, F, D), DN_BETA * F ** -0.5),
        "attn_w_qkv": nrm((N_ATTN, D, 3 * D), D ** -0.5),
        "attn_w_o": nrm((N_ATTN, D, D), DN_BETA * D ** -0.5),
        "attn_lambda": nrm((N_ATTN, 4, DA_HEAD_DIM), 0.1),
        "attn_subln_g": 1.0 + nrm((N_ATTN, 2 * DA_HEAD_DIM), 0.1),
        "hy_w_in": nrm((N_HYENA, D, 3 * D), D ** -0.5),
        "hy_b_in": nrm((N_HYENA, 3 * D), 0.01),
        "hy_conv_w": nrm((N_HYENA, HY_SHORT, 3 * D), HY_SHORT ** -0.5),
        "hy_conv_b": nrm((N_HYENA, 3 * D), 0.01),
        "hy_filt_w_in": nrm((N_HYENA, HY_EMB, HY_FILTER_W), HY_EMB ** -0.5),
        "hy_filt_w_mid": nrm((N_HYENA, 2, HY_FILTER_W, HY_FILTER_W), HY_FILTER_W ** -0.5),
        "hy_filt_b": nrm((N_HYENA, 3, HY_FILTER_W), 0.1),
        "hy_filt_freq": 1.0 + nrm((N_HYENA, HY_FILTER_W), 0.1),
        "hy_filt_w_out": nrm((N_HYENA, HY_FILTER_W, 2 * D), 0.1 * HY_FILTER_W ** -0.5),
        "hy_bias": nrm((N_HYENA, D), 1.0),
        "hy_w_out": nrm((N_HYENA, D, D), DN_BETA * D ** -0.5),
        "hy_b_out": nrm((N_HYENA, D), 0.01),
        "ssm_w_in": nrm((N_SSM, D, SSM_IN_DIM), D ** -0.5),
        "ssm_conv_w": nrm((N_SSM, SSM_CONV, SSM_CONV_DIM), SSM_CONV ** -0.5),
        "ssm_conv_b": nrm((N_SSM, SSM_CONV_DIM), 0.01),
        "ssm_dt_bias": dt0 + jnp.log(-jnp.expm1(-dt0)),
        "ssm_a_log": jnp.log(jax.random.uniform(next(ks), (N_SSM, 2, SSM_HEADS), f32, 1.0, 16.0)),
        "ssm_d": 1.0 + nrm((N_SSM, SSM_HEADS), 0.1),
        "ssm_norm_g": 1.0 + nrm((N_SSM, SSM_D_INNER), 0.1),
        "ssm_w_out": nrm((N_SSM, SSM_D_INNER, D), DN_BETA * SSM_D_INNER ** -0.5),
    }


def reference(x, c, ctx, c_ctx, ada_w, ada_b, ln_g, ln_b, ffn_w13, ffn_w2,
              attn_w_qkv, attn_w_o, attn_lambda, attn_subln_g,
              hy_w_in, hy_b_in, hy_conv_w, hy_conv_b, hy_filt_w_in, hy_filt_w_mid, hy_filt_b,
              hy_filt_freq, hy_filt_w_out, hy_bias, hy_w_out, hy_b_out,
              ssm_w_in, ssm_conv_w, ssm_conv_b, ssm_dt_bias, ssm_a_log, ssm_d, ssm_norm_g, ssm_w_out):
    lat, cx = x, ctx
    s_lat = jax.nn.silu(c)
    s_ctx = jax.nn.silu(c_ctx)
    for i in range(DEPTH):
        kind, j = i % N_MIXERS, i // N_MIXERS
        ctx_out = i < DEPTH - 1
        ctx_in = ctx_out or kind != 1
        ml = jnp.split((s_lat @ ada_w[i] + ada_b[i])[:, None, :], N_MOD, axis=-1)
        mc = jnp.split((s_ctx @ ada_w[i] + ada_b[i])[None, None, :], N_MOD, axis=-1)

        lat = post_norm(lat, swiglu(modulate(lat, ml[0], ml[1]), ffn_w13[i, 0], ffn_w2[i, 0]),
                        ml[2], FFN_RES_W, ln_g[i, 0], ln_b[i, 0])
        if ctx_in:
            cx = post_norm(cx, swiglu(modulate(cx, mc[0], mc[1]), ffn_w13[i, 0], ffn_w2[i, 0]),
                           mc[2], FFN_RES_W, ln_g[i, 0], ln_b[i, 0])

        hl = modulate(lat, ml[3], ml[4])
        hc = modulate(cx, mc[3], mc[4]) if ctx_in else None
        if kind == 0:
            yc, yl = diff_attention(hc, hl, attn_w_qkv[j], attn_w_o[j], attn_lambda[j], attn_subln_g[j], i, ctx_out)
        elif kind == 1:
            hy_args = (hy_w_in[j], hy_b_in[j], hy_conv_w[j], hy_conv_b[j], hy_filt_w_in[j], hy_filt_w_mid[j],
                       hy_filt_b[j], hy_filt_freq[j], hy_filt_w_out[j], hy_bias[j], hy_w_out[j], hy_b_out[j])
            yl = hyena_mixer(hl, *hy_args)
            yc = hyena_mixer(hc, *hy_args) if ctx_out else None
        else:
            yc, yl = ssm_mixer(hc, hl, ssm_w_in[j], ssm_conv_w[j], ssm_conv_b[j], ssm_dt_bias[j],
                               ssm_a_log[j], ssm_d[j], ssm_norm_g[j], ssm_w_out[j], ctx_out)
        lat = post_norm(lat, yl, ml[5], 1.0, ln_g[i, 1], ln_b[i, 1])
        if ctx_out:
            cx = post_norm(cx, yc, mc[5], 1.0, ln_g[i, 1], ln_b[i, 1])

        lat = post_norm(lat, swiglu(modulate(lat, ml[6], ml[7]), ffn_w13[i, 1], ffn_w2[i, 1]),
                        ml[8], FFN_RES_W, ln_g[i, 2], ln_b[i, 2])
        if ctx_out:
            cx = post_norm(cx, swiglu(modulate(cx, mc[6], mc[7]), ffn_w13[i, 1], ffn_w2[i, 1]),
                           mc[8], FFN_RES_W, ln_g[i, 2], ln_b[i, 2])
    return lat
```

```python
import functools
import math

import jax
import jax.numpy as jnp
from jax import lax
from jax.experimental import pallas as pl
from jax.experimental.pallas import tpu as pltpu

F32 = jnp.float32
BF16 = jnp.bfloat16

D_MODEL = 1024
BATCH = 8
SEQ = 2048
DEPTH = 4
GRID_W = 64
CTX_LEN = 256
N_MIXERS = 3
DN_ALPHA = (2 * DEPTH) ** 0.25
LN_EPS = 1e-5
FFN_RES_W = 0.5
N_MOD = 9
D_FF = 2816
DA_HEADS = 8
DA_HEAD_DIM = D_MODEL // DA_HEADS // 2
DA_SCALE = DA_HEAD_DIM ** -0.5
Q_BLOCK = 128
ROPE_BASE = 10000.0
ROPE_AXIS_DIM = DA_HEAD_DIM // 2
ROPE_PAIRS = ROPE_AXIS_DIM // 2
HY_SHORT = 3
HY_EMB = 33
HY_BANDS = (HY_EMB - 1) // 2
HY_FILTER_W = 64
HY_TARGET = 1e-2
HY_MIN_DECAY = math.log(HY_TARGET) / 0.3
HY_MAX_DECAY = math.log(HY_TARGET) / 1.5
HY_SHIFT = 0.05
SSM_D_INNER = 2 * D_MODEL
SSM_HEAD_DIM = 64
SSM_HEADS = SSM_D_INNER // SSM_HEAD_DIM
SSM_GROUPS = 4
SSM_STATE = 128
SSM_CONV = 3
SSM_CHUNK = 128
SSM_GN = SSM_GROUPS * SSM_STATE
SSM_CONV_DIM = SSM_D_INNER + 2 * SSM_GN
SSM_IN_DIM = SSM_D_INNER + SSM_CONV_DIM + 2 * SSM_HEADS

N_ROWS = BATCH + 1
MOD_ROWS = 16
VMEM_LIMIT = 56 * 1024 * 1024


def _ada_kernel(c_ref, w_ref, b_ref, o_ref):
    c = c_ref[...]
    s = (c * jax.nn.sigmoid(c)).astype(BF16)
    o_ref[0] = jnp.dot(s, w_ref[0].astype(BF16), preferred_element_type=F32) + b_ref[0]


def _ada_all(c_rows, ada_w, ada_b):
    tn = 1024
    return pl.pallas_call(
        _ada_kernel,
        grid=(DEPTH, N_MOD * D_MODEL // tn),
        in_specs=[
            pl.BlockSpec((MOD_ROWS, D_MODEL), lambda i, j: (0, 0)),
            pl.BlockSpec((1, D_MODEL, tn), lambda i, j: (i, 0, j)),
            pl.BlockSpec((1, 1, tn), lambda i, j: (i, 0, j)),
        ],
        out_specs=pl.BlockSpec((1, MOD_ROWS, tn), lambda i, j: (i, 0, j)),
        out_shape=jax.ShapeDtypeStruct((DEPTH, MOD_ROWS, N_MOD * D_MODEL), F32),
        compiler_params=pltpu.CompilerParams(dimension_semantics=("arbitrary", "arbitrary")),
        name="ada_mod",
    )(c_rows, ada_w, ada_b.reshape(DEPTH, 1, N_MOD * D_MODEL))


FFN_TM = 1024
FFN_TF = 256


def _layer_norm_rows(z, g, b):
    mu = jnp.mean(z, axis=-1, keepdims=True)
    zc = z - mu
    var = jnp.mean(zc * zc, axis=-1, keepdims=True)
    return zc * lax.rsqrt(var + LN_EPS) * g + b


def _ffn_kernel(x_ref, mod_ref, w13_ref, w2_ref, lng_ref, lnb_ref, o_ref, h_ref, acc_ref):
    x = x_ref[0]
    m = mod_ref[0]
    h_ref[...] = (x * (1.0 + m[1:2]) + m[0:1]).astype(BF16)
    acc_ref[...] = jnp.zeros_like(acc_ref)

    def body(j, carry):
        off = pl.multiple_of(j * FFN_TF, FFN_TF)
        off_u = pl.multiple_of(D_FF + j * FFN_TF, FFN_TF)
        h = h_ref[...]
        a = jnp.dot(h, w13_ref[:, pl.ds(off, FFN_TF)], preferred_element_type=F32)
        u = jnp.dot(h, w13_ref[:, pl.ds(off_u, FFN_TF)], preferred_element_type=F32)
        g = (a * jax.nn.sigmoid(a) * u).astype(BF16)
        acc_ref[...] += jnp.dot(g, w2_ref[pl.ds(off, FFN_TF), :], preferred_element_type=F32)
        return carry

    lax.fori_loop(0, D_FF // FFN_TF, body, 0)
    z = DN_ALPHA * x + (FFN_RES_W * (1.0 + m[2:3])) * acc_ref[...]
    o_ref[0] = _layer_norm_rows(z, lng_ref[...], lnb_ref[...])


def _ffn(xs, mod3, w13, w2, ln_g, ln_b):
    rows = xs.shape[0]
    const = dict(pipeline_mode=pl.Buffered(1))
    return pl.pallas_call(
        _ffn_kernel,
        grid=(rows, SEQ // FFN_TM),
        in_specs=[
            pl.BlockSpec((1, FFN_TM, D_MODEL), lambda r, t: (r, t, 0)),
            pl.BlockSpec((1, 3, D_MODEL), lambda r, t: (r, 0, 0)),
            pl.BlockSpec((D_MODEL, 2 * D_FF), lambda r, t: (0, 0), **const),
            pl.BlockSpec((D_FF, D_MODEL), lambda r, t: (0, 0), **const),
            pl.BlockSpec((1, D_MODEL), lambda r, t: (0, 0)),
            pl.BlockSpec((1, D_MODEL), lambda r, t: (0, 0)),
        ],
        out_specs=pl.BlockSpec((1, FFN_TM, D_MODEL), lambda r, t: (r, t, 0)),
        out_shape=jax.ShapeDtypeStruct(xs.shape, F32),
        scratch_shapes=[pltpu.VMEM((FFN_TM, D_MODEL), BF16), pltpu.VMEM((FFN_TM, D_MODEL), F32)],
        compiler_params=pltpu.CompilerParams(
            dimension_semantics=("arbitrary", "arbitrary"), vmem_limit_bytes=VMEM_LIMIT),
        name="ffn",
    )(xs, mod3, w13, w2, ln_g.reshape(1, D_MODEL), ln_b.reshape(1, D_MODEL))


def layer_norm(x, g, b):
    mu = jnp.mean(x, -1, keepdims=True)
    var = jnp.mean(jnp.square(x - mu), -1, keepdims=True)
    return (x - mu) * lax.rsqrt(var + LN_EPS) * g + b


def rms_norm(x, g):
    return x * lax.rsqrt(jnp.mean(x * x, -1, keepdims=True) + LN_EPS) * g


def modulate(x, shift, scale):
    return x * (1.0 + scale) + shift


def post_norm(x, y, gate, res_w, g, b):
    return layer_norm(DN_ALPHA * x + res_w * (1.0 + gate) * y, g, b)


def dwconv_centred(u, w, b):
    k = w.shape[0]
    pad = k // 2
    L = u.shape[1]
    up = jnp.pad(u, ((0, 0), (pad, pad), (0, 0)))
    out = b + up[:, 0:L] * w[0]
    for t in range(1, k):
        out = out + up[:, t:t + L] * w[t]
    return out


def axial_rope(L):
    rows = L // GRID_W
    row = jnp.broadcast_to(jnp.arange(rows, dtype=jnp.float32)[:, None], (rows, GRID_W)).reshape(L)
    col = jnp.broadcast_to(jnp.arange(GRID_W, dtype=jnp.float32)[None, :], (rows, GRID_W)).reshape(L)
    inv = ROPE_BASE ** (-jnp.arange(ROPE_PAIRS, dtype=jnp.float32) / ROPE_PAIRS)
    ang = jnp.stack([row[:, None] * inv, col[:, None] * inv], axis=1)
    ang = jnp.concatenate([ang, ang], -1)
    return jnp.cos(ang), jnp.sin(ang)


def apply_rope(x, cos, sin):
    sh = x.shape
    xs = x.reshape(sh[:-1] + (2, ROPE_AXIS_DIM))
    x1, x2 = jnp.split(xs, 2, axis=-1)
    rot = jnp.concatenate([-x2, x1], -1)
    c = cos[None, :, None, None].astype(x.dtype)
    s = sin[None, :, None, None].astype(x.dtype)
    return (xs * c + rot * s).reshape(sh)


def diff_qkv(h, w_qkv):
    b, L = h.shape[0], h.shape[1]
    q, k, v = jnp.split(h @ w_qkv, 3, axis=-1)
    return (q.reshape(b, L, DA_HEADS, 2, DA_HEAD_DIM),
            k.reshape(b, L, DA_HEADS, 2, DA_HEAD_DIM),
            v.reshape(b, L, DA_HEADS, 2 * DA_HEAD_DIM))


def diff_attend(q, k, v, lam_full):
    s = jnp.einsum('bqhtd,bkhtd->bhtqk', q, k).astype(jnp.float32) * DA_SCALE
    p = jax.nn.softmax(s, axis=-1)
    a = p[:, :, 0] - lam_full * p[:, :, 1]
    return jnp.einsum('bhqk,bkhe->bqhe', a.astype(v.dtype), v)


def diff_out(o, subln_g, lam_init, w_o):
    b, L = o.shape[0], o.shape[1]
    o = rms_norm(o, subln_g) * (1.0 - lam_init)
    return o.reshape(b, L, D_MODEL) @ w_o


def diff_attention(hc, hl, w_qkv, w_o, lam, subln_g, layer_idx, ctx_out):
    b, L = hl.shape[0], hl.shape[1]
    lam_init = 0.8 - 0.6 * math.exp(-0.3 * layer_idx)
    lf = lam.astype(jnp.float32)
    lam_full = jnp.exp(jnp.sum(lf[0] * lf[1])) - jnp.exp(jnp.sum(lf[2] * lf[3])) + lam_init
    qc, kc, vc = diff_qkv(hc, w_qkv)
    ql, kl, vl = diff_qkv(hl, w_qkv)
    cos, sin = axial_rope(L)
    ql = apply_rope(ql, cos, sin)
    kl = apply_rope(kl, cos, sin)
    k_all = jnp.concatenate([kc, kl], axis=1)
    v_all = jnp.concatenate([vc, vl], axis=1)
    nb = L // Q_BLOCK
    qb = ql.reshape(b, nb, Q_BLOCK, DA_HEADS, 2, DA_HEAD_DIM).transpose(1, 0, 2, 3, 4, 5)
    ob = lax.map(lambda q: diff_attend(q, k_all, v_all, lam_full), qb)
    ol = ob.transpose(1, 0, 2, 3, 4).reshape(b, L, DA_HEADS, 2 * DA_HEAD_DIM)
    yl = diff_out(ol, subln_g, lam_init, w_o)
    yc = diff_out(diff_attend(qc, kc, vc, lam_full), subln_g, lam_init, w_o) if ctx_out else None
    return yc, yl


def hyena_filter(L, f_w_in, f_w_mid, f_b, f_freq, f_w_out):
    f32 = jnp.float32
    t = jnp.linspace(0.0, 1.0, L, dtype=f32)[:, None]
    w = 2.0 * math.pi * jnp.arange(L, dtype=f32)[:, None] / L
    f = jnp.linspace(1e-4, HY_BANDS - 1, HY_BANDS, dtype=f32)
    z = jnp.concatenate([t, jnp.cos(f * w), -jnp.sin(f * w)], axis=-1)
    freq = f_freq.astype(f32)
    h = jnp.sin(freq * (z @ f_w_in.astype(f32) + f_b[0].astype(f32)))
    h = jnp.sin(freq * (h @ f_w_mid[0].astype(f32) + f_b[1].astype(f32)))
    h = jnp.sin(freq * (h @ f_w_mid[1].astype(f32) + f_b[2].astype(f32)))
    h = (h @ f_w_out.astype(f32)).reshape(L, 2, D_MODEL)
    deltas = jnp.abs(jnp.linspace(HY_MIN_DECAY, HY_MAX_DECAY, D_MODEL, dtype=f32))
    window = jnp.exp(-t * deltas) + HY_SHIFT
    h = h * window[:, None, :]
    h_fwd, h_bwd = h[:, 0], h[:, 1]
    return jnp.concatenate([h_fwd, jnp.zeros((1, D_MODEL), f32), h_bwd[:0:-1]], axis=0)


def long_conv_bidir(u, k_full, bias):
    L = u.shape[1]
    uf32 = u.astype(jnp.float32)
    uf = jnp.fft.rfft(uf32, n=2 * L, axis=1)
    kf = jnp.fft.rfft(k_full, n=2 * L, axis=0)
    y = jnp.fft.irfft(uf * kf[None], n=2 * L, axis=1)[:, :L]
    return (y + uf32 * bias.astype(jnp.float32)).astype(u.dtype)


def hyena_mixer(h, w_in, b_in, conv_w, conv_b, f_w_in, f_w_mid, f_b, f_freq, f_w_out, bias, w_out, b_out):
    L = h.shape[1]
    k_full = hyena_filter(L, f_w_in, f_w_mid, f_b, f_freq, f_w_out)
    u = dwconv_centred(h @ w_in + b_in, conv_w, conv_b)
    x0, x1, v = jnp.split(u, 3, axis=-1)
    v = long_conv_bidir(v * x1, k_full, bias)
    return (v * x0) @ w_out + b_out


def segsum(a):
    T = a.shape[-1]
    cs = jnp.cumsum(a, axis=-1)
    d = cs[..., :, None] - cs[..., None, :]
    mask = jnp.tril(jnp.ones((T, T), dtype=bool))
    return jnp.where(mask, d, -jnp.inf)


def ssd_scan(xdt, adt, bm, cm, init):
    b, L, h, p = xdt.shape
    g, n = bm.shape[2], bm.shape[3]
    r = h // g
    nc = L // SSM_CHUNK
    X = xdt.reshape(b, nc, SSM_CHUNK, g, r, p)
    A = adt.reshape(b, nc, SSM_CHUNK, g, r).transpose(0, 3, 4, 1, 2)
    Bc = bm.reshape(b, nc, SSM_CHUNK, g, n)
    Cc = cm.reshape(b, nc, SSM_CHUNK, g, n)
    A_cs = jnp.cumsum(A, axis=-1)
    Lmat = jnp.exp(segsum(A))
    CB = jnp.einsum('bclgn,bcsgn->bgcls', Cc, Bc)
    y_diag = jnp.einsum('bgrcls,bcsgrp->bclgrp', CB[:, :, None] * Lmat, X)
    decay_states = jnp.exp(A_cs[..., -1:] - A_cs).transpose(0, 3, 4, 1, 2)
    states = jnp.einsum('bclgn,bclgrp->bcgrpn', Bc, X * decay_states[..., None])
    states = jnp.concatenate([init.reshape(b, 1, g, r, p, n), states], axis=1)
    chunk_decay = jnp.exp(segsum(jnp.pad(A_cs[..., -1], ((0, 0), (0, 0), (0, 0), (1, 0)))))
    states = jnp.einsum('bgrzc,bcgrpn->bzgrpn', chunk_decay, states)
    prev, final = states[:, :-1], states[:, -1]
    y_off = jnp.einsum('bclgn,bcgrpn->bclgrp', Cc, prev) * jnp.exp(A_cs).transpose(0, 3, 4, 1, 2)[..., None]
    return (y_diag + y_off).reshape(b, L, h, p), final.reshape(b, h, p, n)


def ssm_scan(h, w_in, conv_w, conv_b, dt_bias, a_log, d_skip, init_f, init_b):
    f32 = jnp.float32
    b, L = h.shape[0], h.shape[1]
    zxbcdt = h @ w_in
    z = zxbcdt[..., :SSM_D_INNER]
    xbc = zxbcdt[..., SSM_D_INNER:SSM_D_INNER + SSM_CONV_DIM]
    dt = zxbcdt[..., SSM_D_INNER + SSM_CONV_DIM:].reshape(b, L, 2, SSM_HEADS)
    xbc = jax.nn.silu(dwconv_centred(xbc, conv_w, conv_b)).astype(f32)
    xs = xbc[..., :SSM_D_INNER].reshape(b, L, SSM_HEADS, SSM_HEAD_DIM)
    bm = xbc[..., SSM_D_INNER:SSM_D_INNER + SSM_GN].reshape(b, L, SSM_GROUPS, SSM_STATE)
    cm = xbc[..., SSM_D_INNER + SSM_GN:].reshape(b, L, SSM_GROUPS, SSM_STATE)
    dt = jax.nn.softplus(dt.astype(f32) + dt_bias.astype(f32))
    a = -jnp.exp(a_log.astype(f32))
    y_f, s_f = ssd_scan(xs * dt[:, :, 0, :, None], a[0] * dt[:, :, 0], bm, cm, init_f)
    y_b, s_b = ssd_scan(jnp.flip(xs * dt[:, :, 1, :, None], 1), jnp.flip(a[1] * dt[:, :, 1], 1),
                        jnp.flip(bm, 1), jnp.flip(cm, 1), init_b)
    y = y_f + jnp.flip(y_b, 1) + xs * d_skip.astype(f32)[:, None]
    return y.reshape(b, L, SSM_D_INNER), z, s_f, s_b


def ssm_out(y, z, norm_g, w_out):
    b, L = y.shape[0], y.shape[1]
    yg = (y * jax.nn.silu(z.astype(jnp.float32))).reshape(b, L, SSM_GROUPS, SSM_D_INNER // SSM_GROUPS)
    yg = yg * lax.rsqrt(jnp.mean(yg * yg, -1, keepdims=True) + LN_EPS)
    return (yg.reshape(b, L, SSM_D_INNER).astype(z.dtype) * norm_g) @ w_out


def ssm_mixer(hc, hl, w_in, conv_w, conv_b, dt_bias, a_log, d_skip, norm_g, w_out, ctx_out):
    zero = jnp.zeros((hl.shape[0], SSM_HEADS, SSM_HEAD_DIM, SSM_STATE), jnp.float32)
    yc_in, zc, s_f, s_b = ssm_scan(hc, w_in, conv_w, conv_b, dt_bias, a_log, d_skip, zero, zero)
    yl_in, zl, _, _ = ssm_scan(hl, w_in, conv_w, conv_b, dt_bias, a_log, d_skip, s_f, s_b)
    yl = ssm_out(yl_in, zl, norm_g, w_out)
    yc = ssm_out(yc_in, zc, norm_g, w_out) if ctx_out else None
    return yc, yl


def kernel(x, c, ctx, c_ctx, ada_w, ada_b, ln_g, ln_b, ffn_w13, ffn_w2, attn_w_qkv, attn_w_o, attn_lambda, attn_subln_g, hy_w_in, hy_b_in, hy_conv_w, hy_conv_b, hy_filt_w_in, hy_filt_w_mid, hy_filt_b, hy_filt_freq, hy_filt_w_out, hy_bias, hy_w_out, hy_b_out, ssm_w_in, ssm_conv_w, ssm_conv_b, ssm_dt_bias, ssm_a_log, ssm_d, ssm_norm_g, ssm_w_out):
    c_rows = jnp.concatenate(
        [c, c_ctx[None], jnp.zeros((MOD_ROWS - N_ROWS, D_MODEL), F32)], axis=0)
    mod = _ada_all(c_rows, ada_w, ada_b)[:, :N_ROWS].reshape(DEPTH, N_ROWS, N_MOD, D_MODEL)
    w13_bf = ffn_w13.astype(BF16)
    w2_bf = ffn_w2.astype(BF16)

    s = jnp.concatenate([x, ctx.reshape(1, SEQ, D_MODEL)], axis=0)
    for i in range(DEPTH):
        kind, j = i % N_MIXERS, i // N_MIXERS
        ctx_out = i < DEPTH - 1
        m = mod[i]
        s = _ffn(s, m[:, 0:3], w13_bf[i, 0], w2_bf[i, 0], ln_g[i, 0], ln_b[i, 0])

        lat = s[:BATCH]
        cx = s[BATCH].reshape(BATCH, CTX_LEN, D_MODEL)
        ml = [m[:BATCH, t][:, None, :] for t in range(N_MOD)]
        mc = [m[BATCH, t][None, None, :] for t in range(N_MOD)]
        hl = modulate(lat, ml[3], ml[4])
        hc = modulate(cx, mc[3], mc[4])
        if kind == 0:
            yc, yl = diff_attention(hc, hl, attn_w_qkv[j], attn_w_o[j], attn_lambda[j], attn_subln_g[j], i, ctx_out)
        elif kind == 1:
            hy_args = (hy_w_in[j], hy_b_in[j], hy_conv_w[j], hy_conv_b[j], hy_filt_w_in[j], hy_filt_w_mid[j],
                       hy_filt_b[j], hy_filt_freq[j], hy_filt_w_out[j], hy_bias[j], hy_w_out[j], hy_b_out[j])
            yl = hyena_mixer(hl, *hy_args)
            yc = hyena_mixer(hc, *hy_args) if ctx_out else None
        else:
            yc, yl = ssm_mixer(hc, hl, ssm_w_in[j], ssm_conv_w[j], ssm_conv_b[j], ssm_dt_bias[j],
                               ssm_a_log[j], ssm_d[j], ssm_norm_g[j], ssm_w_out[j], ctx_out)
        lat = post_norm(lat, yl, ml[5], 1.0, ln_g[i, 1], ln_b[i, 1])
        if ctx_out:
            cx = post_norm(cx, yc, mc[5], 1.0, ln_g[i, 1], ln_b[i, 1])
            s = jnp.concatenate([lat, cx.reshape(1, SEQ, D_MODEL)], axis=0)
            m2 = m
        else:
            s = lat
            m2 = m[:BATCH]
        s = _ffn(s, m2[:, 6:9], w13_bf[i, 1], w2_bf[i, 1], ln_g[i, 2], ln_b[i, 2])
    return s[:BATCH]
```

```python
import functools
import math

import jax
import jax.numpy as jnp
from jax import lax
from jax.experimental import pallas as pl
from jax.experimental.pallas import tpu as pltpu

F32 = jnp.float32
BF16 = jnp.bfloat16

D_MODEL = 1024
BATCH = 8
SEQ = 2048
DEPTH = 4
GRID_W = 64
CTX_LEN = 256
N_MIXERS = 3
DN_ALPHA = (2 * DEPTH) ** 0.25
LN_EPS = 1e-5
FFN_RES_W = 0.5
N_MOD = 9
D_FF = 2816
DA_HEADS = 8
DA_HEAD_DIM = D_MODEL // DA_HEADS // 2
DA_SCALE = DA_HEAD_DIM ** -0.5
Q_BLOCK = 128
ROPE_BASE = 10000.0
ROPE_AXIS_DIM = DA_HEAD_DIM // 2
ROPE_PAIRS = ROPE_AXIS_DIM // 2
HY_SHORT = 3
HY_EMB = 33
HY_BANDS = (HY_EMB - 1) // 2
HY_FILTER_W = 64
HY_TARGET = 1e-2
HY_MIN_DECAY = math.log(HY_TARGET) / 0.3
HY_MAX_DECAY = math.log(HY_TARGET) / 1.5
HY_SHIFT = 0.05
SSM_D_INNER = 2 * D_MODEL
SSM_HEAD_DIM = 64
SSM_HEADS = SSM_D_INNER // SSM_HEAD_DIM
SSM_GROUPS = 4
SSM_STATE = 128
SSM_CONV = 3
SSM_CHUNK = 128
SSM_GN = SSM_GROUPS * SSM_STATE
SSM_CONV_DIM = SSM_D_INNER + 2 * SSM_GN
SSM_IN_DIM = SSM_D_INNER + SSM_CONV_DIM + 2 * SSM_HEADS

N_ROWS = BATCH + 1
MOD_ROWS = 16
VMEM_LIMIT = 56 * 1024 * 1024


def _ada_kernel(c_ref, w_ref, b_ref, o_ref):
    c = c_ref[...]
    s = (c * jax.nn.sigmoid(c)).astype(BF16)
    o_ref[0] = jnp.dot(s, w_ref[0].astype(BF16), preferred_element_type=F32) + b_ref[0]


def _ada_all(c_rows, ada_w, ada_b):
    tn = 1024
    return pl.pallas_call(
        _ada_kernel,
        grid=(DEPTH, N_MOD * D_MODEL // tn),
        in_specs=[
            pl.BlockSpec((MOD_ROWS, D_MODEL), lambda i, j: (0, 0)),
            pl.BlockSpec((1, D_MODEL, tn), lambda i, j: (i, 0, j)),
            pl.BlockSpec((1, 1, tn), lambda i, j: (i, 0, j)),
        ],
        out_specs=pl.BlockSpec((1, MOD_ROWS, tn), lambda i, j: (i, 0, j)),
        out_shape=jax.ShapeDtypeStruct((DEPTH, MOD_ROWS, N_MOD * D_MODEL), F32),
        compiler_params=pltpu.CompilerParams(dimension_semantics=("arbitrary", "arbitrary")),
        name="ada_mod",
    )(c_rows, ada_w, ada_b.reshape(DEPTH, 1, N_MOD * D_MODEL))


FFN_TM = 1024
FFN_TF = 256


def _layer_norm_rows(z, g, b):
    mu = jnp.mean(z, axis=-1, keepdims=True)
    zc = z - mu
    var = jnp.mean(zc * zc, axis=-1, keepdims=True)
    return zc * lax.rsqrt(var + LN_EPS) * g + b


def _ffn_kernel(x_ref, mod_ref, w13_ref, w2_ref, lng_ref, lnb_ref, o_ref, h_ref, acc_ref):
    x = x_ref[0]
    m = mod_ref[0]
    h_ref[...] = (x * (1.0 + m[1:2]) + m[0:1]).astype(BF16)
    acc_ref[...] = jnp.zeros_like(acc_ref)

    def body(j, carry):
        off = pl.multiple_of(j * FFN_TF, FFN_TF)
        off_u = pl.multiple_of(D_FF + j * FFN_TF, FFN_TF)
        h = h_ref[...]
        a = jnp.dot(h, w13_ref[:, pl.ds(off, FFN_TF)], preferred_element_type=F32)
        u = jnp.dot(h, w13_ref[:, pl.ds(off_u, FFN_TF)], preferred_element_type=F32)
        g = (a * jax.nn.sigmoid(a) * u).astype(BF16)
        acc_ref[...] += jnp.dot(g, w2_ref[pl.ds(off, FFN_TF), :], preferred_element_type=F32)
        return carry

    lax.fori_loop(0, D_FF // FFN_TF, body, 0)
    z = DN_ALPHA * x + (FFN_RES_W * (1.0 + m[2:3])) * acc_ref[...]
    o_ref[0] = _layer_norm_rows(z, lng_ref[...], lnb_ref[...])


def _ffn(xs, mod3, w13, w2, ln_g, ln_b):
    rows = xs.shape[0]
    const = dict(pipeline_mode=pl.Buffered(1))
    return pl.pallas_call(
        _ffn_kernel,
        grid=(rows, SEQ // FFN_TM),
        in_specs=[
            pl.BlockSpec((1, FFN_TM, D_MODEL), lambda r, t: (r, t, 0)),
            pl.BlockSpec((1, 3, D_MODEL), lambda r, t: (r, 0, 0)),
            pl.BlockSpec((D_MODEL, 2 * D_FF), lambda r, t: (0, 0), **const),
            pl.BlockSpec((D_FF, D_MODEL), lambda r, t: (0, 0), **const),
            pl.BlockSpec((1, D_MODEL), lambda r, t: (0, 0)),
            pl.BlockSpec((1, D_MODEL), lambda r, t: (0, 0)),
        ],
        out_specs=pl.BlockSpec((1, FFN_TM, D_MODEL), lambda r, t: (r, t, 0)),
        out_shape=jax.ShapeDtypeStruct(xs.shape, F32),
        scratch_shapes=[pltpu.VMEM((FFN_TM, D_MODEL), BF16), pltpu.VMEM((FFN_TM, D_MODEL), F32)],
        compiler_params=pltpu.CompilerParams(
            dimension_semantics=("arbitrary", "arbitrary"), vmem_limit_bytes=VMEM_LIMIT),
        name="ffn",
    )(xs, mod3, w13, w2, ln_g.reshape(1, D_MODEL), ln_b.reshape(1, D_MODEL))


LOG2E = 1.4426950408889634
HEAD_W = 2 * DA_HEAD_DIM
QKV_TM = 512
ATTN_TQ = 512
ATTN_SUB = 128


def _rope_tables(seq):
    pos = jnp.arange(seq, dtype=jnp.int32)
    row = (pos // GRID_W).astype(F32)
    col = (pos % GRID_W).astype(F32)
    inv = ROPE_BASE ** (-jnp.arange(ROPE_PAIRS, dtype=F32) / ROPE_PAIRS)
    ang_r = row[:, None] * inv
    ang_c = col[:, None] * inv
    ang = jnp.concatenate([ang_r, ang_r, ang_c, ang_c], axis=-1)
    sign = jnp.concatenate([-jnp.ones(ROPE_PAIRS), jnp.ones(ROPE_PAIRS)] * 2).astype(F32)
    cos = jnp.tile(jnp.cos(ang), (1, 2))
    sin = jnp.tile(jnp.sin(ang) * sign, (1, 2))
    return (jnp.stack([cos, jnp.ones_like(cos)]), jnp.stack([sin, jnp.zeros_like(sin)]))


def _qkv_kernel(x_ref, mod_ref, w_ref, cos_ref, sin_ref, q_ref, k_ref, v_ref):
    x = x_ref[0]
    m = mod_ref[0]
    h = (x * (1.0 + m[1:2]) + m[0:1]).astype(BF16)
    cos = cos_ref[0]
    sin = sin_ref[0]
    lane = lax.broadcasted_iota(jnp.int32, cos.shape, 1)
    first_half = (lane & (ROPE_AXIS_DIM - 1)) < ROPE_PAIRS

    def rope(blk):
        partner = jnp.where(first_half, pltpu.roll(blk, HEAD_W - ROPE_PAIRS, 1),
                            pltpu.roll(blk, ROPE_PAIRS, 1))
        return blk * cos + partner * sin

    q = jnp.dot(h, w_ref[:, 0:D_MODEL], preferred_element_type=F32)
    for j in range(DA_HEADS):
        sl = slice(j * HEAD_W, (j + 1) * HEAD_W)
        q_ref[0, :, sl] = (rope(q[:, sl]) * (DA_SCALE * LOG2E)).astype(BF16)
    k = jnp.dot(h, w_ref[:, D_MODEL:2 * D_MODEL], preferred_element_type=F32)
    for j in range(DA_HEADS):
        sl = slice(j * HEAD_W, (j + 1) * HEAD_W)
        k_ref[0, :, sl] = rope(k[:, sl]).astype(BF16)
    v_ref[0] = jnp.dot(h, w_ref[:, 2 * D_MODEL:], preferred_element_type=F32).astype(BF16)


def _qkv(xs, mod2, w_qkv, cos_tab, sin_tab):
    rows, seq, _ = xs.shape
    n_lat = rows - 1
    tok = pl.BlockSpec((1, QKV_TM, D_MODEL), lambda r, t: (r, t, 0))
    tab = pl.BlockSpec((1, QKV_TM, HEAD_W), lambda r, t: (r // n_lat, t, 0))
    out = jax.ShapeDtypeStruct(xs.shape, BF16)
    return pl.pallas_call(
        _qkv_kernel,
        grid=(rows, seq // QKV_TM),
        in_specs=[
            tok,
            pl.BlockSpec((1, 2, D_MODEL), lambda r, t: (r, 0, 0)),
            pl.BlockSpec((D_MODEL, 3 * D_MODEL), lambda r, t: (0, 0), pipeline_mode=pl.Buffered(1)),
            tab, tab,
        ],
        out_specs=[tok, tok, tok],
        out_shape=[out, out, out],
        compiler_params=pltpu.CompilerParams(
            dimension_semantics=("arbitrary", "arbitrary"), vmem_limit_bytes=VMEM_LIMIT),
        name="attn_qkv",
    )(xs, mod2, w_qkv, cos_tab, sin_tab)


def _attn_kernel(lam_ref, g_ref, q_ref, *refs, n_src, lam_init):
    k_refs, v_refs = refs[:n_src], refs[n_src:2 * n_src]
    o_ref, kcat, vcat = refs[2 * n_src:]

    @pl.when(pl.program_id(2) == 0)
    def _():
        off = 0
        for kr, vr in zip(k_refs, v_refs):
            n = kr.shape[1]
            kcat[off:off + n, :] = kr[0]
            vcat[off:off + n, :] = vr[0]
            off += n

    lam4 = lam_ref[...]
    lam_full = (jnp.exp(jnp.sum(lam4[0:1] * lam4[1:2], axis=-1, keepdims=True))
                - jnp.exp(jnp.sum(lam4[2:3] * lam4[3:4], axis=-1, keepdims=True)) + lam_init)
    k = kcat[...]
    v = vcat[...]
    nt = (((1,), (1,)), ((), ()))
    gain = g_ref[...] * (1.0 - lam_init)

    def softmax_av(qm):
        s = lax.dot_general(qm, k, nt, preferred_element_type=F32)
        e = jnp.exp2(s - jnp.max(s, axis=-1, keepdims=True))
        l = jnp.sum(e, axis=-1, keepdims=True)
        return jnp.dot(e.astype(BF16), v, preferred_element_type=F32) * (1.0 / l)

    tq = q_ref.shape[1]
    sub = min(ATTN_SUB, tq)
    for i in range(tq // sub):
        q = q_ref[0, i * sub:(i + 1) * sub, :]
        lane = lax.broadcasted_iota(jnp.int32, q.shape, 1)
        o = (softmax_av(jnp.where(lane < DA_HEAD_DIM, q, jnp.zeros_like(q)))
             - lam_full * softmax_av(jnp.where(lane >= DA_HEAD_DIM, q, jnp.zeros_like(q))))
        o = o * lax.rsqrt(jnp.mean(o * o, axis=-1, keepdims=True) + LN_EPS)
        o_ref[0, i * sub:(i + 1) * sub, :] = (o * gain).astype(BF16)


def _attend(lam, subln_g, q, k, v, lam_init, q_row0, n_q_rows, q_len, srcs):
    _, seq, _ = q.shape
    per_row = seq // q_len
    n_seq = n_q_rows * per_row
    tq = min(ATTN_TQ, q_len)
    nqt = q_len // tq
    n_keys = sum(s[2] for s in srcs)
    qspec = pl.BlockSpec((1, tq, HEAD_W),
                         lambda n, h, t: (q_row0 + n // per_row, (n % per_row) * nqt + t, h))
    kv_specs = [pl.BlockSpec((1, ln, HEAD_W), functools.partial(
        lambda n, h, t, rf, bf: (rf(n), bf(n), h), rf=rf, bf=bf)) for rf, bf, ln in srcs]
    ospec = pl.BlockSpec((1, tq, HEAD_W), lambda n, h, t: (n // per_row, (n % per_row) * nqt + t, h))
    return pl.pallas_call(
        functools.partial(_attn_kernel, n_src=len(srcs), lam_init=lam_init),
        grid=(n_seq, DA_HEADS, nqt),
        in_specs=[
            pl.BlockSpec((4, DA_HEAD_DIM), lambda n, h, t: (0, 0)),
            pl.BlockSpec((1, HEAD_W), lambda n, h, t: (0, 0)),
            qspec,
        ] + kv_specs + kv_specs,
        out_specs=ospec,
        out_shape=jax.ShapeDtypeStruct((n_q_rows, seq, D_MODEL), BF16),
        scratch_shapes=[pltpu.VMEM((n_keys, HEAD_W), BF16), pltpu.VMEM((n_keys, HEAD_W), BF16)],
        compiler_params=pltpu.CompilerParams(
            dimension_semantics=("arbitrary", "arbitrary", "arbitrary"), vmem_limit_bytes=VMEM_LIMIT),
        name="attn_core",
    )(lam, subln_g.reshape(1, HEAD_W), q, *([k] * len(srcs)), *([v] * len(srcs)))


PROJ_TM = 1024


def _proj_norm_kernel(x_ref, y_ref, gate_ref, w_ref, b_ref, lng_ref, lnb_ref, o_ref):
    y = jnp.dot(y_ref[0], w_ref[...], preferred_element_type=F32) + b_ref[...]
    z = DN_ALPHA * x_ref[0] + (1.0 + gate_ref[0]) * y
    o_ref[0] = _layer_norm_rows(z, lng_ref[...], lnb_ref[...])


def _proj_norm(xs, ys, gate, w, b, ln_g, ln_b):
    rows, seq, _ = xs.shape
    kdim = ys.shape[-1]
    vec = pl.BlockSpec((1, D_MODEL), lambda r, t: (0, 0))
    return pl.pallas_call(
        _proj_norm_kernel,
        grid=(rows, seq // PROJ_TM),
        in_specs=[
            pl.BlockSpec((1, PROJ_TM, D_MODEL), lambda r, t: (r, t, 0)),
            pl.BlockSpec((1, PROJ_TM, kdim), lambda r, t: (r, t, 0)),
            pl.BlockSpec((1, 1, D_MODEL), lambda r, t: (r, 0, 0)),
            pl.BlockSpec((kdim, D_MODEL), lambda r, t: (0, 0), pipeline_mode=pl.Buffered(1)),
            vec, vec, vec,
        ],
        out_specs=pl.BlockSpec((1, PROJ_TM, D_MODEL), lambda r, t: (r, t, 0)),
        out_shape=jax.ShapeDtypeStruct(xs.shape, F32),
        compiler_params=pltpu.CompilerParams(
            dimension_semantics=("arbitrary", "arbitrary"), vmem_limit_bytes=VMEM_LIMIT),
        name="proj_norm",
    )(xs, ys, gate, w, b.reshape(1, D_MODEL), ln_g.reshape(1, D_MODEL), ln_b.reshape(1, D_MODEL))


def _attention_layer(s, m, w_qkv, w_o, lam, subln_g, ln_g, ln_b, layer_idx, ctx_out):
    rows, seq, _ = s.shape
    n_lat = rows - 1
    ctx_len = seq // n_lat
    lam_init = 0.8 - 0.6 * math.exp(-0.3 * layer_idx)
    cos_tab, sin_tab = _rope_tables(seq)
    q, k, v = _qkv(s, m[:, 3:5], w_qkv.astype(BF16), cos_tab, sin_tab)
    ctx_src = (lambda n: n_lat, lambda n: n, ctx_len)
    o = _attend(lam, subln_g, q, k, v, lam_init, 0, n_lat, seq,
                [ctx_src, (lambda n: n, lambda n: 0, seq)])
    if ctx_out:
        oc = _attend(lam, subln_g, q, k, v, lam_init, n_lat, 1, ctx_len, [ctx_src])
        o = jnp.concatenate([o, oc], axis=0)
    else:
        s, m = s[:n_lat], m[:n_lat]
    return _proj_norm(s, o, m[:, 5:6], w_o.astype(BF16), jnp.zeros((D_MODEL,), F32), ln_g, ln_b)


MODPROJ_TM = 512


def _mod_proj_kernel(x_ref, mod_ref, w_ref, b_ref, o_ref):
    m = mod_ref[0]
    h = (x_ref[0] * (1.0 + m[1:2]) + m[0:1]).astype(BF16)
    o_ref[0] = (jnp.dot(h, w_ref[...], preferred_element_type=F32) + b_ref[...]).astype(o_ref.dtype)


def _mod_proj(xs, mod2, w, b, out_dtype):
    rows, seq, _ = xs.shape
    n = w.shape[1]
    return pl.pallas_call(
        _mod_proj_kernel,
        grid=(rows, seq // MODPROJ_TM),
        in_specs=[
            pl.BlockSpec((1, MODPROJ_TM, D_MODEL), lambda r, t: (r, t, 0)),
            pl.BlockSpec((1, 2, D_MODEL), lambda r, t: (r, 0, 0)),
            pl.BlockSpec((D_MODEL, n), lambda r, t: (0, 0), pipeline_mode=pl.Buffered(1)),
            pl.BlockSpec((1, n), lambda r, t: (0, 0)),
        ],
        out_specs=pl.BlockSpec((1, MODPROJ_TM, n), lambda r, t: (r, t, 0)),
        out_shape=jax.ShapeDtypeStruct((rows, seq, n), out_dtype),
        compiler_params=pltpu.CompilerParams(
            dimension_semantics=("arbitrary", "arbitrary"), vmem_limit_bytes=VMEM_LIMIT),
        name="mod_proj",
    )(xs, mod2, w, b.reshape(1, n))


HY_PAD = 128
HY_TD = 256


def _dft_tables(length):
    n = 2 * length
    idx = jnp.arange(length, dtype=jnp.int32)
    ang = ((idx[:, None] * idx[None, :]) % n).astype(F32) * (2.0 * math.pi / n)
    return jnp.cos(ang).astype(BF16), jnp.sin(ang).astype(BF16)


def _hyena_tables(length):
    t = jnp.linspace(0.0, 1.0, length, dtype=F32)[:, None]
    w = 2.0 * math.pi * jnp.arange(length, dtype=F32)[:, None] / length
    f = jnp.linspace(1e-4, HY_BANDS - 1, HY_BANDS, dtype=F32)
    z = jnp.concatenate([t, jnp.cos(f * w), -jnp.sin(f * w)], axis=-1)
    deltas = jnp.abs(jnp.linspace(HY_MIN_DECAY, HY_MAX_DECAY, D_MODEL, dtype=F32))
    window = jnp.exp(-t * deltas) + HY_SHIFT
    return jnp.pad(z, ((0, 0), (0, HY_PAD - HY_EMB))), window


def _split_dot(mat_ref, x):
    hi = x.astype(BF16)
    lo = (x - hi.astype(F32)).astype(BF16)
    mat = mat_ref[...]
    return (jnp.dot(mat, hi, preferred_element_type=F32) + jnp.dot(mat, lo, preferred_element_type=F32))


def _hy_filter_kernel(z_ref, win_ref, wmid_ref, b_ref, freq_ref, woutf_ref, woutb_ref, window_ref,
                      c_ref, s_ref, kc_ref, ks_ref, kn_ref):
    exact = dict(precision=lax.Precision.HIGHEST, preferred_element_type=F32)
    freq = freq_ref[...]
    b = b_ref[...]
    h = jnp.sin(freq * (jnp.dot(z_ref[...], win_ref[...], **exact) + b[0:1]))
    h = jnp.sin(freq * (jnp.dot(h, wmid_ref[0], **exact) + b[1:2]))
    h = jnp.sin(freq * (jnp.dot(h, wmid_ref[1], **exact) + b[2:3]))
    window = window_ref[...]
    row = lax.broadcasted_iota(jnp.int32, window.shape, 0)
    hf = jnp.dot(h, woutf_ref[...], **exact) * window
    hb = jnp.where(row == 0, 0.0, jnp.dot(h, woutb_ref[...], **exact) * window)
    n = 2 * window.shape[0]
    wgt = jnp.where(row == 0, 1.0 / n, 2.0 / n)
    ssum = hf + hb
    kc_ref[...] = wgt * _split_dot(c_ref, ssum)
    ks_ref[...] = wgt * _split_dot(s_ref, hf - hb)
    sign = jnp.where((row & 1) == 0, 1.0, -1.0)
    kn_ref[...] = jnp.sum(ssum * sign, axis=0, keepdims=True) * (1.0 / n)


def _hy_filter(length, f_w_in, f_w_mid, f_b, f_freq, f_w_out, cmat, smat):
    z, window = _hyena_tables(length)
    pad_w = HY_PAD - HY_FILTER_W
    w_in = jnp.pad(f_w_in, ((0, HY_PAD - HY_EMB), (0, pad_w)))
    w_mid = jnp.pad(f_w_mid, ((0, 0), (0, pad_w), (0, pad_w)))
    b = jnp.pad(f_b, ((0, 8 - f_b.shape[0]), (0, pad_w)))
    freq = jnp.pad(f_freq, (0, pad_w)).reshape(1, HY_PAD)
    w_out = jnp.pad(f_w_out, ((0, pad_w), (0, 0)))
    full = lambda shape: pl.BlockSpec(shape, lambda c: (0,) * len(shape))
    ncb = D_MODEL // HY_TD
    col = pl.BlockSpec((length, HY_TD), lambda c: (0, c))
    mat = pl.BlockSpec((length, length), lambda c: (0, 0), pipeline_mode=pl.Buffered(1))
    return pl.pallas_call(
        _hy_filter_kernel,
        grid=(ncb,),
        in_specs=[
            full((length, HY_PAD)), full((HY_PAD, HY_PAD)), full((2, HY_PAD, HY_PAD)), full((8, HY_PAD)),
            full((1, HY_PAD)),
            pl.BlockSpec((HY_PAD, HY_TD), lambda c: (0, c)),
            pl.BlockSpec((HY_PAD, HY_TD), lambda c: (0, ncb + c)),
            col, mat, mat,
        ],
        out_specs=[col, col, pl.BlockSpec((1, HY_TD), lambda c: (0, c))],
        out_shape=[jax.ShapeDtypeStruct((length, D_MODEL), F32), jax.ShapeDtypeStruct((length, D_MODEL), F32),
                   jax.ShapeDtypeStruct((1, D_MODEL), F32)],
        compiler_params=pltpu.CompilerParams(
            dimension_semantics=("arbitrary",), vmem_limit_bytes=VMEM_LIMIT),
        name="hy_filter",
    )(z, w_in, w_mid, b, freq, w_out, w_out, window, cmat, smat)


def _hy_conv_kernel(x0_ref, x1_ref, v_ref, cw_ref, cb_ref, kc_ref, ks_ref, kn_ref, bias_ref,
                    c_ref, s_ref, o_ref):
    length, td = kc_ref.shape
    row = lax.broadcasted_iota(jnp.int32, (length, td), 0)
    cw = cw_ref[...]
    cb = cb_ref[...]

    def short_conv(u, p):
        prev = jnp.where(row == 0, 0.0, pltpu.roll(u, 1, 0))
        nxt = jnp.where(row == length - 1, 0.0, pltpu.roll(u, length - 1, 0))
        return cb[p:p + 1] + prev * cw[0, p:p + 1] + u * cw[1, p:p + 1] + nxt * cw[2, p:p + 1]

    vv = short_conv(v_ref[0], 2) * short_conv(x1_ref[0], 1)
    vvb = vv.astype(BF16)
    cmat = c_ref[...]
    smat = s_ref[...]
    vc = jnp.dot(cmat, vvb, preferred_element_type=F32)
    vs = jnp.dot(smat, vvb, preferred_element_type=F32)
    kc = kc_ref[...]
    ks = ks_ref[...]
    yc = (vc * kc - vs * ks).astype(BF16)
    ys = (vc * ks + vs * kc).astype(BF16)
    y = jnp.dot(cmat, yc, preferred_element_type=F32) + jnp.dot(smat, ys, preferred_element_type=F32)
    sign = jnp.where((row & 1) == 0, 1.0, -1.0)
    nyq = jnp.sum(vv * sign, axis=0, keepdims=True) * kn_ref[...]
    y = y + sign * nyq + vv * bias_ref[...]
    o_ref[0] = (y * short_conv(x0_ref[0], 0)).astype(BF16)


def _hy_conv(u, length, row0, n_rows, conv_w, conv_b, kc, ks, kn, bias, cmat, smat):
    _, seq, _ = u.shape
    per_row = seq // length
    ncb = D_MODEL // HY_TD
    part = lambda p: pl.BlockSpec(
        (1, length, HY_TD), lambda c, n: (row0 + n // per_row, n % per_row, p * ncb + c))
    col = pl.BlockSpec((length, HY_TD), lambda c, n: (0, c), pipeline_mode=pl.Buffered(1))
    vec = pl.BlockSpec((1, HY_TD), lambda c, n: (0, c))
    mat = pl.BlockSpec((length, length), lambda c, n: (0, 0), pipeline_mode=pl.Buffered(1))
    return pl.pallas_call(
        _hy_conv_kernel,
        grid=(ncb, n_rows * per_row),
        in_specs=[
            part(0), part(1), part(2),
            pl.BlockSpec((HY_SHORT, 3, HY_TD), lambda c, n: (0, 0, c)),
            pl.BlockSpec((3, HY_TD), lambda c, n: (0, c)),
            col, col, vec, vec, mat, mat,
        ],
        out_specs=pl.BlockSpec((1, length, HY_TD), lambda c, n: (n // per_row, n % per_row, c)),
        out_shape=jax.ShapeDtypeStruct((n_rows, seq, D_MODEL), BF16),
        compiler_params=pltpu.CompilerParams(
            dimension_semantics=("arbitrary", "arbitrary"), vmem_limit_bytes=VMEM_LIMIT),
        name="hy_conv",
    )(u, u, u, conv_w.reshape(HY_SHORT, 3, D_MODEL), conv_b.reshape(3, D_MODEL), kc, ks, kn,
      bias.reshape(1, D_MODEL), cmat, smat)


def _hyena_layer(s, m, w_in, b_in, conv_w, conv_b, f_w_in, f_w_mid, f_b, f_freq, f_w_out, bias,
                 w_out, b_out, ln_g, ln_b, ctx_out):
    rows, seq, _ = s.shape
    n_lat = rows - 1
    if not ctx_out:
        s, m = s[:n_lat], m[:n_lat]
    u = _mod_proj(s, m[:, 3:5], w_in.astype(BF16), b_in, F32)
    filt = (f_w_in, f_w_mid, f_b, f_freq, f_w_out)

    def mix(length, row0, n_rows):
        cmat, smat = _dft_tables(length)
        kc, ks, kn = _hy_filter(length, *filt, cmat, smat)
        return _hy_conv(u, length, row0, n_rows, conv_w, conv_b, kc, ks, kn, bias, cmat, smat)

    y = mix(seq, 0, n_lat)
    if ctx_out:
        y = jnp.concatenate([y, mix(seq // n_lat, n_lat, 1)], axis=0)
    return _proj_norm(s, y, m[:, 5:6], w_out.astype(BF16), b_out, ln_g, ln_b)


SSD_HG = SSM_HEADS // SSM_GROUPS
SSD_GW = SSD_HG * SSM_HEAD_DIM
SSD_LANES = 128
SSD_IN_COLS = SSM_D_INNER + SSM_CONV_DIM + SSM_GROUPS * SSD_LANES


def _split3(x):
    hi = x.astype(BF16)
    r = x - hi.astype(F32)
    mid = r.astype(BF16)
    return hi, mid, (r - mid.astype(F32)).astype(BF16)


def _exact_lhs_dot(sel, x):
    return sum(jnp.dot(sel, t, preferred_element_type=F32) for t in _split3(x))


def _exact_rhs_dot(x, sel):
    return sum(jnp.dot(t, sel, preferred_element_type=F32) for t in _split3(x))


def _ssd_kernel(z_ref, x_ref, b_ref, c_ref, dt_ref, cwx_ref, cbx_ref, cwb_ref, cbb_ref, cwc_ref, cbc_ref,
                dtb_ref, alog_ref, dskip_ref, ng_ref, expand_ref, initf_ref, initb_ref,
                o_ref, finf_ref, finb_ref, xs_ref, bs_ref, cs_ref, a_ref, dts_ref, y_ref, st_ref):
    length = x_ref.shape[1]
    q = SSM_CHUNK
    nc = length // q

    def conv_silu(u, cw, cb):
        row = lax.broadcasted_iota(jnp.int32, u.shape, 0)
        prev = jnp.where(row == 0, 0.0, pltpu.roll(u, 1, 0))
        nxt = jnp.where(row == length - 1, 0.0, pltpu.roll(u, length - 1, 0))
        v = cb + prev * cw[0:1] + u * cw[1:2] + nxt * cw[2:3]
        return v * jax.nn.sigmoid(v)

    xs_ref[...] = conv_silu(x_ref[0], cwx_ref[...], cbx_ref[...])
    bs_ref[...] = conv_silu(b_ref[0], cwb_ref[...], cbb_ref[...])
    cs_ref[...] = conv_silu(c_ref[0], cwc_ref[...], cbc_ref[...])
    pre = dt_ref[0] + dtb_ref[0]
    dt = jnp.maximum(pre, 0.0) + jnp.log(1.0 + jnp.exp(-jnp.abs(pre)))
    dts_ref[...] = dt
    a_ref[...] = dt * (-jnp.exp(alog_ref[0]))

    ri = lax.broadcasted_iota(jnp.int32, (q, q), 0)
    ci = lax.broadcasted_iota(jnp.int32, (q, q), 1)
    nt = (((1,), (1,)), ((), ()))

    for d, (init_ref, fin_ref) in enumerate(((initf_ref, finf_ref), (initb_ref, finb_ref))):
        fwd = d == 0
        mask = (ri >= ci) if fwd else (ri <= ci)
        tri = jnp.where(mask, 1.0, 0.0).astype(BF16)
        end = q - 1 if fwd else 0
        expand = expand_ref[d]
        st_ref[...] = init_ref[0, 0]

        def body(i, carry, fwd=fwd, mask=mask, tri=tri, end=end, expand=expand, d=d):
            c = i if fwd else nc - 1 - i
            r0 = pl.multiple_of(c * q, q)
            rows = pl.ds(r0, q)
            cum = _exact_lhs_dot(tri, a_ref[rows, :])
            cum_t = cum.T
            xd = xs_ref[rows, :] * _exact_rhs_dot(dts_ref[rows, :], expand)
            bc = bs_ref[rows, :]
            cc = cs_ref[rows, :].astype(BF16)
            cb = lax.dot_general(cc, bc.astype(BF16), nt, preferred_element_type=F32)
            st = st_ref[...]
            dec_in = _exact_rhs_dot(jnp.exp(cum), expand)
            y = jnp.dot(cc, st.astype(BF16), preferred_element_type=F32) * dec_in
            ys = []
            for j in range(SSD_HG):
                lane = d * SSD_HG + j
                seg = cum[:, lane:lane + 1] - cum_t[lane:lane + 1, :]
                lmat = jnp.where(mask, jnp.exp(jnp.minimum(seg, 0.0)), 0.0)
                xj = xd[:, j * SSM_HEAD_DIM:(j + 1) * SSM_HEAD_DIM].astype(BF16)
                ys.append(jnp.dot((cb * lmat).astype(BF16), xj, preferred_element_type=F32))
            y = y + jnp.concatenate(ys, axis=1)
            if fwd:
                y_ref[rows, :] = y
            else:
                y_ref[rows, :] += y
            dec_out = _exact_rhs_dot(jnp.exp(cum[end:end + 1, :] - cum), expand)
            upd = jnp.dot(bc.T.astype(BF16), (xd * dec_out).astype(BF16), preferred_element_type=F32)
            st_ref[...] = st * dec_in[end:end + 1, :] + upd
            return carry

        lax.fori_loop(0, nc, body, 0)
        fin_ref[0, 0] = st_ref[...]

    z = z_ref[0]
    yg = (y_ref[...] + xs_ref[...] * dskip_ref[0]) * (z * jax.nn.sigmoid(z))
    yn = yg * lax.rsqrt(jnp.mean(yg * yg, axis=-1, keepdims=True) + LN_EPS)
    o_ref[0] = (yn * ng_ref[0]).astype(BF16)


def _ssd(zx, length, row0, n_rows, conv_w, conv_b, dtb, alog, dskip, norm_g, expand, init_f, init_b):
    _, seq, _ = zx.shape
    per_row = seq // length
    n_seq = n_rows * per_row
    q = SSM_CHUNK

    def cols(width, first):
        return pl.BlockSpec((1, length, width),
                            lambda n, g: (row0 + n // per_row, n % per_row, first // width + g))

    def wcols(nrow, width, first):
        return pl.BlockSpec((nrow, width), lambda n, g: (0, first // width + g))

    x0 = SSM_D_INNER
    b0 = 2 * SSM_D_INNER
    c0 = b0 + SSM_GN
    d0 = c0 + SSM_GN
    pergroup = lambda width: pl.BlockSpec((1, 1, width), lambda n, g: (g, 0, 0))
    state = pl.BlockSpec((1, 1, SSM_STATE, SSD_GW), lambda n, g: (n, g, 0, 0))
    state_shape = jax.ShapeDtypeStruct((n_seq, SSM_GROUPS, SSM_STATE, SSD_GW), F32)
    cb2 = conv_b.reshape(1, SSM_CONV_DIM)
    return pl.pallas_call(
        _ssd_kernel,
        grid=(n_seq, SSM_GROUPS),
        in_specs=[
            cols(SSD_GW, 0), cols(SSD_GW, x0), cols(SSM_STATE, b0), cols(SSM_STATE, c0), cols(SSD_LANES, d0),
            wcols(SSM_CONV, SSD_GW, 0), wcols(1, SSD_GW, 0),
            wcols(SSM_CONV, SSM_STATE, SSM_D_INNER), wcols(1, SSM_STATE, SSM_D_INNER),
            wcols(SSM_CONV, SSM_STATE, SSM_D_INNER + SSM_GN), wcols(1, SSM_STATE, SSM_D_INNER + SSM_GN),
            pergroup(SSD_LANES), pergroup(SSD_LANES), pergroup(SSD_GW), pergroup(SSD_GW),
            pl.BlockSpec((2, SSD_LANES, SSD_GW), lambda n, g: (0, 0, 0)),
            state, state,
        ],
        out_specs=[
            pl.BlockSpec((1, length, SSD_GW), lambda n, g: (n // per_row, n % per_row, g)),
            state, state,
        ],
        out_shape=[jax.ShapeDtypeStruct((n_rows, seq, SSM_D_INNER), BF16), state_shape, state_shape],
        scratch_shapes=[
            pltpu.VMEM((length, SSD_GW), F32), pltpu.VMEM((length, SSM_STATE), F32),
            pltpu.VMEM((length, SSM_STATE), F32), pltpu.VMEM((length, SSD_LANES), F32),
            pltpu.VMEM((length, SSD_LANES), F32), pltpu.VMEM((length, SSD_GW), F32),
            pltpu.VMEM((SSM_STATE, SSD_GW), F32),
        ],
        compiler_params=pltpu.CompilerParams(
            dimension_semantics=("arbitrary", "arbitrary"), vmem_limit_bytes=VMEM_LIMIT),
        name="ssd_core",
    )(zx, zx, zx, zx, zx, conv_w, cb2, conv_w, cb2, conv_w, cb2,
      dtb.reshape(SSM_GROUPS, 1, SSD_LANES), alog.reshape(SSM_GROUPS, 1, SSD_LANES),
      dskip.reshape(SSM_GROUPS, 1, SSD_GW), norm_g.reshape(SSM_GROUPS, 1, SSD_GW), expand, init_f, init_b)


def _regroup_heads(a, pad_value=0.0):
    lead = a.shape[:-2]
    a = a.reshape(lead + (2, SSM_GROUPS, SSD_HG))
    a = jnp.moveaxis(a, -3, -2).reshape(lead + (SSM_GROUPS, 2 * SSD_HG))
    pad = [(0, 0)] * (a.ndim - 1) + [(0, SSD_LANES - 2 * SSD_HG)]
    return jnp.pad(a, pad, constant_values=pad_value)


def _ssm_layer(s, m, w_in, conv_w, conv_b, dt_bias, a_log, d_skip, norm_g, w_out, ln_g, ln_b, ctx_out):
    rows, seq, _ = s.shape
    n_lat = rows - 1
    ctx_len = seq // n_lat
    n_main = SSM_D_INNER + SSM_CONV_DIM
    w_dt = _regroup_heads(w_in[:, n_main:].reshape(D_MODEL, 2, SSM_HEADS))
    w_all = jnp.concatenate([w_in[:, :n_main], w_dt.reshape(D_MODEL, SSM_GROUPS * SSD_LANES)], axis=1)
    zx = _mod_proj(s, m[:, 3:5], w_all.astype(BF16), jnp.zeros((SSD_IN_COLS,), F32), F32)
    dtb = _regroup_heads(dt_bias)
    alog = _regroup_heads(a_log)
    dskip = jnp.repeat(d_skip, SSM_HEAD_DIM).reshape(SSM_GROUPS, SSD_GW)
    lane = jnp.arange(SSD_LANES)[:, None]
    chan = jnp.arange(SSD_GW)[None, :] // SSM_HEAD_DIM
    expand = jnp.stack([lane == chan, lane == chan + SSD_HG]).astype(BF16)
    common = (conv_w, conv_b, dtb, alog, dskip, norm_g, expand)
    zero = jnp.zeros((n_lat, SSM_GROUPS, SSM_STATE, SSD_GW), F32)
    yc, s_f, s_b = _ssd(zx, ctx_len, n_lat, 1, *common, zero, zero)
    yl, _, _ = _ssd(zx, seq, 0, n_lat, *common, s_f, s_b)
    if ctx_out:
        y = jnp.concatenate([yl, yc], axis=0)
    else:
        y, s, m = yl, s[:n_lat], m[:n_lat]
    return _proj_norm(s, y, m[:, 5:6], w_out.astype(BF16), jnp.zeros((D_MODEL,), F32), ln_g, ln_b)


def kernel(x, c, ctx, c_ctx, ada_w, ada_b, ln_g, ln_b, ffn_w13, ffn_w2, attn_w_qkv, attn_w_o, attn_lambda, attn_subln_g, hy_w_in, hy_b_in, hy_conv_w, hy_conv_b, hy_filt_w_in, hy_filt_w_mid, hy_filt_b, hy_filt_freq, hy_filt_w_out, hy_bias, hy_w_out, hy_b_out, ssm_w_in, ssm_conv_w, ssm_conv_b, ssm_dt_bias, ssm_a_log, ssm_d, ssm_norm_g, ssm_w_out):
    c_rows = jnp.concatenate(
        [c, c_ctx[None], jnp.zeros((MOD_ROWS - N_ROWS, D_MODEL), F32)], axis=0)
    mod = _ada_all(c_rows, ada_w, ada_b)[:, :N_ROWS].reshape(DEPTH, N_ROWS, N_MOD, D_MODEL)
    w13_bf = ffn_w13.astype(BF16)
    w2_bf = ffn_w2.astype(BF16)

    s = jnp.concatenate([x, ctx.reshape(1, SEQ, D_MODEL)], axis=0)
    for i in range(DEPTH):
        kind, j = i % N_MIXERS, i // N_MIXERS
        ctx_out = i < DEPTH - 1
        m = mod[i]
        s = _ffn(s, m[:, 0:3], w13_bf[i, 0], w2_bf[i, 0], ln_g[i, 0], ln_b[i, 0])
        if kind == 0:
            s = _attention_layer(s, m, attn_w_qkv[j], attn_w_o[j], attn_lambda[j], attn_subln_g[j],
                                 ln_g[i, 1], ln_b[i, 1], i, ctx_out)
        elif kind == 1:
            s = _hyena_layer(s, m, hy_w_in[j], hy_b_in[j], hy_conv_w[j], hy_conv_b[j], hy_filt_w_in[j],
                             hy_filt_w_mid[j], hy_filt_b[j], hy_filt_freq[j], hy_filt_w_out[j], hy_bias[j],
                             hy_w_out[j], hy_b_out[j], ln_g[i, 1], ln_b[i, 1], ctx_out)
        else:
            s = _ssm_layer(s, m, ssm_w_in[j], ssm_conv_w[j], ssm_conv_b[j], ssm_dt_bias[j], ssm_a_log[j],
                           ssm_d[j], ssm_norm_g[j], ssm_w_out[j], ln_g[i, 1], ln_b[i, 1], ctx_out)
        m2 = m if ctx_out else m[:BATCH]
        s = _ffn(s, m2[:, 6:9], w13_bf[i, 1], w2_bf[i, 1], ln_g[i, 2], ln_b[i, 2])
    return s[:BATCH]
```

```python
import functools
import math

import jax
import jax.numpy as jnp
from jax import lax
from jax.experimental import pallas as pl
from jax.experimental.pallas import tpu as pltpu

F32 = jnp.float32
BF16 = jnp.bfloat16

D_MODEL = 1024
BATCH = 8
SEQ = 2048
DEPTH = 4
GRID_W = 64
CTX_LEN = 256
N_MIXERS = 3
DN_ALPHA = (2 * DEPTH) ** 0.25
LN_EPS = 1e-5
FFN_RES_W = 0.5
N_MOD = 9
D_FF = 2816
DA_HEADS = 8
DA_HEAD_DIM = D_MODEL // DA_HEADS // 2
DA_SCALE = DA_HEAD_DIM ** -0.5
Q_BLOCK = 128
ROPE_BASE = 10000.0
ROPE_AXIS_DIM = DA_HEAD_DIM // 2
ROPE_PAIRS = ROPE_AXIS_DIM // 2
HY_SHORT = 3
HY_EMB = 33
HY_BANDS = (HY_EMB - 1) // 2
HY_FILTER_W = 64
HY_TARGET = 1e-2
HY_MIN_DECAY = math.log(HY_TARGET) / 0.3
HY_MAX_DECAY = math.log(HY_TARGET) / 1.5
HY_SHIFT = 0.05
SSM_D_INNER = 2 * D_MODEL
SSM_HEAD_DIM = 64
SSM_HEADS = SSM_D_INNER // SSM_HEAD_DIM
SSM_GROUPS = 4
SSM_STATE = 128
SSM_CONV = 3
SSM_CHUNK = 128
SSM_GN = SSM_GROUPS * SSM_STATE
SSM_CONV_DIM = SSM_D_INNER + 2 * SSM_GN
SSM_IN_DIM = SSM_D_INNER + SSM_CONV_DIM + 2 * SSM_HEADS

N_ROWS = BATCH + 1
MOD_ROWS = 16
VMEM_LIMIT = 56 * 1024 * 1024


def _ada_kernel(c_ref, w_ref, b_ref, o_ref):
    c = c_ref[...]
    s = (c * jax.nn.sigmoid(c)).astype(BF16)
    o_ref[0] = jnp.dot(s, w_ref[0].astype(BF16), preferred_element_type=F32) + b_ref[0]


def _ada_all(c_rows, ada_w, ada_b):
    tn = 1024
    return pl.pallas_call(
        _ada_kernel,
        grid=(DEPTH, N_MOD * D_MODEL // tn),
        in_specs=[
            pl.BlockSpec((MOD_ROWS, D_MODEL), lambda i, j: (0, 0)),
            pl.BlockSpec((1, D_MODEL, tn), lambda i, j: (i, 0, j)),
            pl.BlockSpec((1, 1, tn), lambda i, j: (i, 0, j)),
        ],
        out_specs=pl.BlockSpec((1, MOD_ROWS, tn), lambda i, j: (i, 0, j)),
        out_shape=jax.ShapeDtypeStruct((DEPTH, MOD_ROWS, N_MOD * D_MODEL), F32),
        compiler_params=pltpu.CompilerParams(dimension_semantics=("arbitrary", "arbitrary")),
        name="ada_mod",
    )(c_rows, ada_w, ada_b.reshape(DEPTH, 1, N_MOD * D_MODEL))


FFN_TM = 1024
FFN_TF = 256


def _layer_norm_rows(z, g, b):
    mu = jnp.mean(z, axis=-1, keepdims=True)
    zc = z - mu
    var = jnp.mean(zc * zc, axis=-1, keepdims=True)
    return zc * lax.rsqrt(var + LN_EPS) * g + b


def _ffn_kernel(x_ref, mod_ref, w13_ref, w2_ref, lng_ref, lnb_ref, o_ref, g_ref):
    x = x_ref[0]
    m = mod_ref[0]
    h = (x * (1.0 + m[1:2]) + m[0:1]).astype(BF16)
    for j in range(D_FF // FFN_TF):
        au = jnp.dot(h, w13_ref[:, 2 * j * FFN_TF:2 * (j + 1) * FFN_TF], preferred_element_type=F32)
        a, u = au[:, :FFN_TF], au[:, FFN_TF:]
        g_ref[:, j * FFN_TF:(j + 1) * FFN_TF] = (a * jax.nn.sigmoid(a) * u).astype(BF16)
    y = jnp.dot(g_ref[...], w2_ref[...], preferred_element_type=F32)
    z = DN_ALPHA * x + (FFN_RES_W * (1.0 + m[2:3])) * y
    o_ref[0] = _layer_norm_rows(z, lng_ref[...], lnb_ref[...])


def _ffn(xs, mod3, w13, w2, ln_g, ln_b):
    rows = xs.shape[0]
    const = dict(pipeline_mode=pl.Buffered(1))
    return pl.pallas_call(
        _ffn_kernel,
        grid=(rows, SEQ // FFN_TM),
        in_specs=[
            pl.BlockSpec((1, FFN_TM, D_MODEL), lambda r, t: (r, t, 0)),
            pl.BlockSpec((1, 3, D_MODEL), lambda r, t: (r, 0, 0)),
            pl.BlockSpec((D_MODEL, 2 * D_FF), lambda r, t: (0, 0), **const),
            pl.BlockSpec((D_FF, D_MODEL), lambda r, t: (0, 0), **const),
            pl.BlockSpec((1, D_MODEL), lambda r, t: (0, 0)),
            pl.BlockSpec((1, D_MODEL), lambda r, t: (0, 0)),
        ],
        out_specs=pl.BlockSpec((1, FFN_TM, D_MODEL), lambda r, t: (r, t, 0)),
        out_shape=jax.ShapeDtypeStruct(xs.shape, F32),
        scratch_shapes=[pltpu.VMEM((FFN_TM, D_FF), BF16)],
        compiler_params=pltpu.CompilerParams(
            dimension_semantics=("arbitrary", "arbitrary"), vmem_limit_bytes=VMEM_LIMIT),
        name="ffn",
    )(xs, mod3, w13, w2, ln_g.reshape(1, D_MODEL), ln_b.reshape(1, D_MODEL))


LOG2E = 1.4426950408889634
HEAD_W = 2 * DA_HEAD_DIM
QKV_TM = 512
ATTN_TQ = 1024
ATTN_SUB = 256


def _rope_tables(seq):
    pos = jnp.arange(seq, dtype=jnp.int32)
    row = (pos // GRID_W).astype(F32)
    col = (pos % GRID_W).astype(F32)
    inv = ROPE_BASE ** (-jnp.arange(ROPE_PAIRS, dtype=F32) / ROPE_PAIRS)
    ang_r = row[:, None] * inv
    ang_c = col[:, None] * inv
    ang = jnp.concatenate([ang_r, ang_r, ang_c, ang_c], axis=-1)
    sign = jnp.concatenate([-jnp.ones(ROPE_PAIRS), jnp.ones(ROPE_PAIRS)] * 2).astype(F32)
    cos = jnp.tile(jnp.cos(ang), (1, 2))
    sin = jnp.tile(jnp.sin(ang) * sign, (1, 2))
    return (jnp.stack([cos, jnp.ones_like(cos)]), jnp.stack([sin, jnp.zeros_like(sin)]))


def _qkv_kernel(x_ref, mod_ref, w_ref, cos_ref, sin_ref, q_ref, k_ref, v_ref):
    x = x_ref[0]
    m = mod_ref[0]
    h = (x * (1.0 + m[1:2]) + m[0:1]).astype(BF16)
    cos = cos_ref[0]
    sin = sin_ref[0]
    lane = lax.broadcasted_iota(jnp.int32, cos.shape, 1)
    first_half = (lane & (ROPE_AXIS_DIM - 1)) < ROPE_PAIRS

    def rope(blk):
        partner = jnp.where(first_half, pltpu.roll(blk, HEAD_W - ROPE_PAIRS, 1),
                            pltpu.roll(blk, ROPE_PAIRS, 1))
        return blk * cos + partner * sin

    q = jnp.dot(h, w_ref[:, 0:D_MODEL], preferred_element_type=F32)
    for j in range(DA_HEADS):
        sl = slice(j * HEAD_W, (j + 1) * HEAD_W)
        q_ref[0, :, sl] = (rope(q[:, sl]) * (DA_SCALE * LOG2E)).astype(BF16)
    k = jnp.dot(h, w_ref[:, D_MODEL:2 * D_MODEL], preferred_element_type=F32)
    for j in range(DA_HEADS):
        sl = slice(j * HEAD_W, (j + 1) * HEAD_W)
        k_ref[0, :, sl] = rope(k[:, sl]).astype(BF16)
    v_ref[0] = jnp.dot(h, w_ref[:, 2 * D_MODEL:], preferred_element_type=F32).astype(BF16)


def _qkv(xs, mod2, w_qkv, cos_tab, sin_tab):
    rows, seq, _ = xs.shape
    n_lat = rows - 1
    tok = pl.BlockSpec((1, QKV_TM, D_MODEL), lambda r, t: (r, t, 0))
    tab = pl.BlockSpec((1, QKV_TM, HEAD_W), lambda r, t: (r // n_lat, t, 0))
    out = jax.ShapeDtypeStruct(xs.shape, BF16)
    return pl.pallas_call(
        _qkv_kernel,
        grid=(rows, seq // QKV_TM),
        in_specs=[
            tok,
            pl.BlockSpec((1, 2, D_MODEL), lambda r, t: (r, 0, 0)),
            pl.BlockSpec((D_MODEL, 3 * D_MODEL), lambda r, t: (0, 0), pipeline_mode=pl.Buffered(1)),
            tab, tab,
        ],
        out_specs=[tok, tok, tok],
        out_shape=[out, out, out],
        compiler_params=pltpu.CompilerParams(
            dimension_semantics=("arbitrary", "arbitrary"), vmem_limit_bytes=VMEM_LIMIT),
        name="attn_qkv",
    )(xs, mod2, w_qkv, cos_tab, sin_tab)


def _attn_kernel(lam_ref, g_ref, q_ref, *refs, n_src, lam_init):
    k_refs, v_refs = refs[:n_src], refs[n_src:2 * n_src]
    o_ref, kcat, vcat = refs[2 * n_src:]

    @pl.when(pl.program_id(2) == 0)
    def _():
        off = 0
        for kr, vr in zip(k_refs, v_refs):
            n = kr.shape[1]
            kcat[off:off + n, :] = kr[0]
            vcat[off:off + n, :] = vr[0]
            off += n

    lam4 = lam_ref[...]
    lam_full = (jnp.exp(jnp.sum(lam4[0:1] * lam4[1:2], axis=-1, keepdims=True))
                - jnp.exp(jnp.sum(lam4[2:3] * lam4[3:4], axis=-1, keepdims=True)) + lam_init)
    k = kcat[...]
    v = vcat[...]
    nt = (((1,), (1,)), ((), ()))
    gain = g_ref[...] * (1.0 - lam_init)

    def softmax_av(qm):
        s = lax.dot_general(qm, k, nt, preferred_element_type=F32)
        e = jnp.exp2(s - jnp.max(s, axis=-1, keepdims=True))
        l = jnp.sum(e, axis=-1, keepdims=True)
        return jnp.dot(e.astype(BF16), v, preferred_element_type=F32) * (1.0 / l)

    tq = q_ref.shape[1]
    sub = min(ATTN_SUB, tq)
    for i in range(tq // sub):
        q = q_ref[0, i * sub:(i + 1) * sub, :]
        lane = lax.broadcasted_iota(jnp.int32, q.shape, 1)
        o = (softmax_av(jnp.where(lane < DA_HEAD_DIM, q, jnp.zeros_like(q)))
             - lam_full * softmax_av(jnp.where(lane >= DA_HEAD_DIM, q, jnp.zeros_like(q))))
        o = o * lax.rsqrt(jnp.mean(o * o, axis=-1, keepdims=True) + LN_EPS)
        o_ref[0, i * sub:(i + 1) * sub, :] = (o * gain).astype(BF16)


def _attend(lam, subln_g, q, k, v, lam_init, q_row0, n_q_rows, q_len, srcs):
    _, seq, _ = q.shape
    per_row = seq // q_len
    n_seq = n_q_rows * per_row
    tq = min(ATTN_TQ, q_len)
    nqt = q_len // tq
    n_keys = sum(s[2] for s in srcs)
    qspec = pl.BlockSpec((1, tq, HEAD_W),
                         lambda n, h, t: (q_row0 + n // per_row, (n % per_row) * nqt + t, h))
    kv_specs = [pl.BlockSpec((1, ln, HEAD_W), functools.partial(
        lambda n, h, t, rf, bf: (rf(n), bf(n), h), rf=rf, bf=bf)) for rf, bf, ln in srcs]
    ospec = pl.BlockSpec((1, tq, HEAD_W), lambda n, h, t: (n // per_row, (n % per_row) * nqt + t, h))
    return pl.pallas_call(
        functools.partial(_attn_kernel, n_src=len(srcs), lam_init=lam_init),
        grid=(n_seq, DA_HEADS, nqt),
        in_specs=[
            pl.BlockSpec((4, DA_HEAD_DIM), lambda n, h, t: (0, 0)),
            pl.BlockSpec((1, HEAD_W), lambda n, h, t: (0, 0)),
            qspec,
        ] + kv_specs + kv_specs,
        out_specs=ospec,
        out_shape=jax.ShapeDtypeStruct((n_q_rows, seq, D_MODEL), BF16),
        scratch_shapes=[pltpu.VMEM((n_keys, HEAD_W), BF16), pltpu.VMEM((n_keys, HEAD_W), BF16)],
        compiler_params=pltpu.CompilerParams(
            dimension_semantics=("arbitrary", "arbitrary", "arbitrary"), vmem_limit_bytes=VMEM_LIMIT),
        name="attn_core",
    )(lam, subln_g.reshape(1, HEAD_W), q, *([k] * len(srcs)), *([v] * len(srcs)))


PROJ_TM = 1024


def _proj_norm_kernel(x_ref, y_ref, gate_ref, w_ref, b_ref, lng_ref, lnb_ref, o_ref):
    y = jnp.dot(y_ref[0], w_ref[...], preferred_element_type=F32) + b_ref[...]
    z = DN_ALPHA * x_ref[0] + (1.0 + gate_ref[0]) * y
    o_ref[0] = _layer_norm_rows(z, lng_ref[...], lnb_ref[...])


def _proj_norm(xs, ys, gate, w, b, ln_g, ln_b):
    rows, seq, _ = xs.shape
    kdim = ys.shape[-1]
    vec = pl.BlockSpec((1, D_MODEL), lambda r, t: (0, 0))
    return pl.pallas_call(
        _proj_norm_kernel,
        grid=(rows, seq // PROJ_TM),
        in_specs=[
            pl.BlockSpec((1, PROJ_TM, D_MODEL), lambda r, t: (r, t, 0)),
            pl.BlockSpec((1, PROJ_TM, kdim), lambda r, t: (r, t, 0)),
            pl.BlockSpec((1, 1, D_MODEL), lambda r, t: (r, 0, 0)),
            pl.BlockSpec((kdim, D_MODEL), lambda r, t: (0, 0), pipeline_mode=pl.Buffered(1)),
            vec, vec, vec,
        ],
        out_specs=pl.BlockSpec((1, PROJ_TM, D_MODEL), lambda r, t: (r, t, 0)),
        out_shape=jax.ShapeDtypeStruct(xs.shape, F32),
        compiler_params=pltpu.CompilerParams(
            dimension_semantics=("arbitrary", "arbitrary"), vmem_limit_bytes=VMEM_LIMIT),
        name="proj_norm",
    )(xs, ys, gate, w, b.reshape(1, D_MODEL), ln_g.reshape(1, D_MODEL), ln_b.reshape(1, D_MODEL))


def _attention_layer(s, m, w_qkv, w_o, lam, subln_g, ln_g, ln_b, layer_idx, ctx_out):
    rows, seq, _ = s.shape
    n_lat = rows - 1
    ctx_len = seq // n_lat
    lam_init = 0.8 - 0.6 * math.exp(-0.3 * layer_idx)
    cos_tab, sin_tab = _rope_tables(seq)
    q, k, v = _qkv(s, m[:, 3:5], w_qkv.astype(BF16), cos_tab, sin_tab)
    ctx_src = (lambda n: n_lat, lambda n: n, ctx_len)
    o = _attend(lam, subln_g, q, k, v, lam_init, 0, n_lat, seq,
                [ctx_src, (lambda n: n, lambda n: 0, seq)])
    if ctx_out:
        oc = _attend(lam, subln_g, q, k, v, lam_init, n_lat, 1, ctx_len, [ctx_src])
        o = jnp.concatenate([o, oc], axis=0)
    else:
        s, m = s[:n_lat], m[:n_lat]
    return _proj_norm(s, o, m[:, 5:6], w_o.astype(BF16), jnp.zeros((D_MODEL,), F32), ln_g, ln_b)


MODPROJ_TM = 512


def _mod_proj_kernel(x_ref, mod_ref, w_ref, b_ref, o_ref):
    m = mod_ref[0]
    h = (x_ref[0] * (1.0 + m[1:2]) + m[0:1]).astype(BF16)
    o_ref[0] = (jnp.dot(h, w_ref[...], preferred_element_type=F32) + b_ref[...]).astype(o_ref.dtype)


def _mod_proj(xs, mod2, w, b, out_dtype):
    rows, seq, _ = xs.shape
    n = w.shape[1]
    return pl.pallas_call(
        _mod_proj_kernel,
        grid=(rows, seq // MODPROJ_TM),
        in_specs=[
            pl.BlockSpec((1, MODPROJ_TM, D_MODEL), lambda r, t: (r, t, 0)),
            pl.BlockSpec((1, 2, D_MODEL), lambda r, t: (r, 0, 0)),
            pl.BlockSpec((D_MODEL, n), lambda r, t: (0, 0), pipeline_mode=pl.Buffered(1)),
            pl.BlockSpec((1, n), lambda r, t: (0, 0)),
        ],
        out_specs=pl.BlockSpec((1, MODPROJ_TM, n), lambda r, t: (r, t, 0)),
        out_shape=jax.ShapeDtypeStruct((rows, seq, n), out_dtype),
        compiler_params=pltpu.CompilerParams(
            dimension_semantics=("arbitrary", "arbitrary"), vmem_limit_bytes=VMEM_LIMIT),
        name="mod_proj",
    )(xs, mod2, w, b.reshape(1, n))


HY_PAD = 128
HY_TD = 256
HY_BANDS_SPLIT = 8
HY_MIN_BAND = 256


def _dft_tables(length):
    n = 2 * length
    idx = jnp.arange(length, dtype=jnp.int32)
    ang = ((idx[:, None] * idx[None, :]) % n).astype(F32) * (2.0 * math.pi / n)
    return jnp.cos(ang).astype(BF16), jnp.sin(ang).astype(BF16)


def _hyena_tables(length):
    t = jnp.linspace(0.0, 1.0, length, dtype=F32)[:, None]
    w = 2.0 * math.pi * jnp.arange(length, dtype=F32)[:, None] / length
    f = jnp.linspace(1e-4, HY_BANDS - 1, HY_BANDS, dtype=F32)
    z = jnp.concatenate([t, jnp.cos(f * w), -jnp.sin(f * w)], axis=-1)
    deltas = jnp.abs(jnp.linspace(HY_MIN_DECAY, HY_MAX_DECAY, D_MODEL, dtype=F32))
    window = jnp.exp(-t * deltas) + HY_SHIFT
    return jnp.pad(z, ((0, 0), (0, HY_PAD - HY_EMB))), window


def _split_dot(mat_ref, x):
    hi = x.astype(BF16)
    lo = (x - hi.astype(F32)).astype(BF16)
    mat = mat_ref[...]
    return (jnp.dot(mat, hi, preferred_element_type=F32) + jnp.dot(mat, lo, preferred_element_type=F32))


def _hy_filter_kernel(z_ref, win_ref, wmid_ref, b_ref, freq_ref, woutf_ref, woutb_ref, window_ref,
                      c_ref, s_ref, kc_ref, ks_ref, kn_ref):
    exact = dict(precision=lax.Precision.HIGHEST, preferred_element_type=F32)
    freq = freq_ref[...]
    b = b_ref[...]
    h = jnp.sin(freq * (jnp.dot(z_ref[...], win_ref[...], **exact) + b[0:1]))
    h = jnp.sin(freq * (jnp.dot(h, wmid_ref[0], **exact) + b[1:2]))
    h = jnp.sin(freq * (jnp.dot(h, wmid_ref[1], **exact) + b[2:3]))
    window = window_ref[...]
    row = lax.broadcasted_iota(jnp.int32, window.shape, 0)
    hf = jnp.dot(h, woutf_ref[...], **exact) * window
    hb = jnp.where(row == 0, 0.0, jnp.dot(h, woutb_ref[...], **exact) * window)
    n = 2 * window.shape[0]
    wgt = jnp.where(row == 0, 1.0 / n, 2.0 / n)
    ssum = hf + hb
    kc_ref[...] = wgt * _split_dot(c_ref, ssum)
    ks_ref[...] = wgt * _split_dot(s_ref, hf - hb)
    sign = jnp.where((row & 1) == 0, 1.0, -1.0)
    kn_ref[...] = jnp.sum(ssum * sign, axis=0, keepdims=True) * (1.0 / n)


def _hy_filter(length, f_w_in, f_w_mid, f_b, f_freq, f_w_out, cmat, smat):
    z, window = _hyena_tables(length)
    pad_w = HY_PAD - HY_FILTER_W
    w_in = jnp.pad(f_w_in, ((0, HY_PAD - HY_EMB), (0, pad_w)))
    w_mid = jnp.pad(f_w_mid, ((0, 0), (0, pad_w), (0, pad_w)))
    b = jnp.pad(f_b, ((0, 8 - f_b.shape[0]), (0, pad_w)))
    freq = jnp.pad(f_freq, (0, pad_w)).reshape(1, HY_PAD)
    w_out = jnp.pad(f_w_out, ((0, pad_w), (0, 0)))
    full = lambda shape: pl.BlockSpec(shape, lambda c: (0,) * len(shape))
    ncb = D_MODEL // HY_TD
    col = pl.BlockSpec((length, HY_TD), lambda c: (0, c))
    mat = pl.BlockSpec((length, length), lambda c: (0, 0), pipeline_mode=pl.Buffered(1))
    return pl.pallas_call(
        _hy_filter_kernel,
        grid=(ncb,),
        in_specs=[
            full((length, HY_PAD)), full((HY_PAD, HY_PAD)), full((2, HY_PAD, HY_PAD)), full((8, HY_PAD)),
            full((1, HY_PAD)),
            pl.BlockSpec((HY_PAD, HY_TD), lambda c: (0, c)),
            pl.BlockSpec((HY_PAD, HY_TD), lambda c: (0, ncb + c)),
            col, mat, mat,
        ],
        out_specs=[col, col, pl.BlockSpec((1, HY_TD), lambda c: (0, c))],
        out_shape=[jax.ShapeDtypeStruct((length, D_MODEL), F32), jax.ShapeDtypeStruct((length, D_MODEL), F32),
                   jax.ShapeDtypeStruct((1, D_MODEL), F32)],
        compiler_params=pltpu.CompilerParams(
            dimension_semantics=("arbitrary",), vmem_limit_bytes=VMEM_LIMIT),
        name="hy_filter",
    )(z, w_in, w_mid, b, freq, w_out, w_out, window, cmat, smat)


def _hy_conv_kernel(x0_ref, x1_ref, v_ref, cw_ref, cb_ref, kc_ref, ks_ref, kn_ref, bias_ref,
                    c_ref, s_ref, o_ref):
    length, td = kc_ref.shape
    row = lax.broadcasted_iota(jnp.int32, (length, td), 0)
    cw = cw_ref[...]
    cb = cb_ref[...]

    def short_conv(u, p):
        prev = jnp.where(row == 0, 0.0, pltpu.roll(u, 1, 0))
        nxt = jnp.where(row == length - 1, 0.0, pltpu.roll(u, length - 1, 0))
        return cb[p:p + 1] + prev * cw[0, p:p + 1] + u * cw[1, p:p + 1] + nxt * cw[2, p:p + 1]

    vv = short_conv(v_ref[0], 2) * short_conv(x1_ref[0], 1)
    vvb = vv.astype(BF16)
    y = None
    band = max(length // HY_BANDS_SPLIT, HY_MIN_BAND)
    for f0 in range(0, length, band):
        fr = slice(f0, f0 + band)
        vc = jnp.dot(c_ref[fr, :], vvb, preferred_element_type=F32)
        vs = jnp.dot(s_ref[fr, :], vvb, preferred_element_type=F32)
        kc = kc_ref[fr, :]
        ks = ks_ref[fr, :]
        yc = (vc * kc - vs * ks).astype(BF16)
        ys = (vc * ks + vs * kc).astype(BF16)
        part = (jnp.dot(c_ref[:, fr], yc, preferred_element_type=F32)
                + jnp.dot(s_ref[:, fr], ys, preferred_element_type=F32))
        y = part if y is None else y + part
    sign = jnp.where((row & 1) == 0, 1.0, -1.0)
    nyq = jnp.sum(vv * sign, axis=0, keepdims=True) * kn_ref[...]
    y = y + sign * nyq + vv * bias_ref[...]
    o_ref[0] = (y * short_conv(x0_ref[0], 0)).astype(BF16)


def _hy_conv(u, length, row0, n_rows, conv_w, conv_b, kc, ks, kn, bias, cmat, smat):
    _, seq, _ = u.shape
    per_row = seq // length
    ncb = D_MODEL // HY_TD
    part = lambda p: pl.BlockSpec(
        (1, length, HY_TD), lambda c, n: (row0 + n // per_row, n % per_row, p * ncb + c))
    col = pl.BlockSpec((length, HY_TD), lambda c, n: (0, c), pipeline_mode=pl.Buffered(1))
    vec = pl.BlockSpec((1, HY_TD), lambda c, n: (0, c))
    mat = pl.BlockSpec((length, length), lambda c, n: (0, 0), pipeline_mode=pl.Buffered(1))
    return pl.pallas_call(
        _hy_conv_kernel,
        grid=(ncb, n_rows * per_row),
        in_specs=[
            part(0), part(1), part(2),
            pl.BlockSpec((HY_SHORT, 3, HY_TD), lambda c, n: (0, 0, c)),
            pl.BlockSpec((3, HY_TD), lambda c, n: (0, c)),
            col, col, vec, vec, mat, mat,
        ],
        out_specs=pl.BlockSpec((1, length, HY_TD), lambda c, n: (n // per_row, n % per_row, c)),
        out_shape=jax.ShapeDtypeStruct((n_rows, seq, D_MODEL), BF16),
        compiler_params=pltpu.CompilerParams(
            dimension_semantics=("arbitrary", "arbitrary"), vmem_limit_bytes=VMEM_LIMIT),
        name="hy_conv",
    )(u, u, u, conv_w.reshape(HY_SHORT, 3, D_MODEL), conv_b.reshape(3, D_MODEL), kc, ks, kn,
      bias.reshape(1, D_MODEL), cmat, smat)


def _hyena_layer(s, m, w_in, b_in, conv_w, conv_b, f_w_in, f_w_mid, f_b, f_freq, f_w_out, bias,
                 w_out, b_out, ln_g, ln_b, ctx_out):
    rows, seq, _ = s.shape
    n_lat = rows - 1
    if not ctx_out:
        s, m = s[:n_lat], m[:n_lat]
    u = _mod_proj(s, m[:, 3:5], w_in.astype(BF16), b_in, F32)
    filt = (f_w_in, f_w_mid, f_b, f_freq, f_w_out)

    def mix(length, row0, n_rows):
        cmat, smat = _dft_tables(length)
        kc, ks, kn = _hy_filter(length, *filt, cmat, smat)
        return _hy_conv(u, length, row0, n_rows, conv_w, conv_b, kc, ks, kn, bias, cmat, smat)

    y = mix(seq, 0, n_lat)
    if ctx_out:
        y = jnp.concatenate([y, mix(seq // n_lat, n_lat, 1)], axis=0)
    return _proj_norm(s, y, m[:, 5:6], w_out.astype(BF16), b_out, ln_g, ln_b)


SSD_HG = SSM_HEADS // SSM_GROUPS
SSD_GW = SSD_HG * SSM_HEAD_DIM
SSD_LANES = 128
SSD_IN_COLS = SSM_D_INNER + SSM_CONV_DIM + SSM_GROUPS * SSD_LANES


def _split3(x):
    hi = x.astype(BF16)
    r = x - hi.astype(F32)
    mid = r.astype(BF16)
    return hi, mid, (r - mid.astype(F32)).astype(BF16)


def _exact_lhs_dot(sel, x):
    return sum(jnp.dot(sel, t, preferred_element_type=F32) for t in _split3(x))


def _rhs_dot2(x, sel):
    return sum(jnp.dot(t, sel, preferred_element_type=F32) for t in _split3(x)[:2])


def _ssd_kernel(z_ref, x_ref, b_ref, c_ref, dt_ref, cwx_ref, cbx_ref, cwb_ref, cbb_ref, cwc_ref, cbc_ref,
                dtb_ref, alog_ref, dskip_ref, ng_ref, expand_ref, initf_ref, initb_ref,
                o_ref, finf_ref, finb_ref, xs_ref, bs_ref, cs_ref, a_ref, dts_ref, cum_ref, y_ref, st_ref):
    length = x_ref.shape[1]
    q = SSM_CHUNK
    nc = length // q

    def conv_silu(u, cw, cb):
        row = lax.broadcasted_iota(jnp.int32, u.shape, 0)
        prev = jnp.where(row == 0, 0.0, pltpu.roll(u, 1, 0))
        nxt = jnp.where(row == length - 1, 0.0, pltpu.roll(u, length - 1, 0))
        v = cb + prev * cw[0:1] + u * cw[1:2] + nxt * cw[2:3]
        return v * jax.nn.sigmoid(v)

    xs_ref[...] = conv_silu(x_ref[0], cwx_ref[...], cbx_ref[...])
    bs_ref[...] = conv_silu(b_ref[0], cwb_ref[...], cbb_ref[...])
    cs_ref[...] = conv_silu(c_ref[0], cwc_ref[...], cbc_ref[...])
    pre = dt_ref[0] + dtb_ref[0]
    dt = jnp.maximum(pre, 0.0) + jnp.log(1.0 + jnp.exp(-jnp.abs(pre)))
    dts_ref[...] = dt
    a_ref[...] = dt * (-jnp.exp(alog_ref[0]))

    ri = lax.broadcasted_iota(jnp.int32, (q, q), 0)
    ci = lax.broadcasted_iota(jnp.int32, (q, q), 1)
    nt = (((1,), (1,)), ((), ()))
    tri = jnp.where(ri >= ci, 1.0, 0.0).astype(BF16)
    for c in range(nc):
        cum_ref[c * q:(c + 1) * q, :] = _exact_lhs_dot(tri, a_ref[c * q:(c + 1) * q, :])

    for d, (init_ref, fin_ref) in enumerate(((initf_ref, finf_ref), (initb_ref, finb_ref))):
        fwd = d == 0
        mask = (ri >= ci) if fwd else (ri <= ci)
        end = q - 1 if fwd else 0
        expand = expand_ref[d]
        st_ref[...] = init_ref[0, 0]

        def body(i, carry, fwd=fwd, mask=mask, end=end, expand=expand, d=d):
            c = i if fwd else nc - 1 - i
            rows = pl.ds(pl.multiple_of(c * q, q), q)
            cum = cum_ref[rows, :]
            if not fwd:
                cum = cum[q - 1:q, :] - cum + a_ref[rows, :]
            cum_t = cum.T
            dtc = dts_ref[rows, :]
            dt_t = dtc.T
            ecum = jnp.exp(cum)
            w_e = _rhs_dot2(dtc * jnp.exp(cum[end:end + 1, :] - cum), expand)
            xs = xs_ref[rows, :]
            xsb = xs.astype(BF16)
            bc = bs_ref[rows, :]
            ccf = cs_ref[rows, :]
            cb = lax.dot_general(ccf.astype(BF16), bc.astype(BF16), nt, preferred_element_type=F32)
            st = st_ref[...]
            stb = st.astype(BF16)
            ys = []
            for p in range(SSD_HG // 2):
                rhs = xsb[:, p * SSD_LANES:(p + 1) * SSD_LANES]
                pair = []
                for j in (2 * p, 2 * p + 1):
                    lane = d * SSD_HG + j
                    seg = cum[:, lane:lane + 1] - cum_t[lane:lane + 1, :]
                    lmat = (jnp.where(mask, jnp.exp(jnp.minimum(seg, 0.0)), 0.0)
                            * (cb * dt_t[lane:lane + 1, :]))
                    pair.append(jnp.dot(lmat.astype(BF16), rhs, preferred_element_type=F32))
                ys.append(jnp.where(ci < SSM_HEAD_DIM, pair[0], pair[1]))
            e_e = _rhs_dot2(ecum, expand)
            y = (jnp.concatenate(ys, axis=1)
                 + jnp.dot(ccf.astype(BF16), stb, preferred_element_type=F32) * e_e)
            if fwd:
                y_ref[rows, :] = y
            else:
                y_ref[rows, :] += y
            upd = jnp.dot(bc.T.astype(BF16), (xs * w_e).astype(BF16), preferred_element_type=F32)
            st_ref[...] = st * e_e[end:end + 1, :] + upd
            return carry

        lax.fori_loop(0, nc, body, 0, unroll=2)
        fin_ref[0, 0] = st_ref[...]

    z = z_ref[0]
    yg = (y_ref[...] + xs_ref[...] * dskip_ref[0]) * (z * jax.nn.sigmoid(z))
    yn = yg * lax.rsqrt(jnp.mean(yg * yg, axis=-1, keepdims=True) + LN_EPS)
    o_ref[0] = (yn * ng_ref[0]).astype(BF16)


def _ssd(zx, length, row0, n_rows, conv_w, conv_b, dtb, alog, dskip, norm_g, expand, init_f, init_b):
    _, seq, _ = zx.shape
    per_row = seq // length
    n_seq = n_rows * per_row
    q = SSM_CHUNK

    def cols(width, first):
        return pl.BlockSpec((1, length, width),
                            lambda n, g: (row0 + n // per_row, n % per_row, first // width + g))

    def wcols(nrow, width, first):
        return pl.BlockSpec((nrow, width), lambda n, g: (0, first // width + g))

    x0 = SSM_D_INNER
    b0 = 2 * SSM_D_INNER
    c0 = b0 + SSM_GN
    d0 = c0 + SSM_GN
    pergroup = lambda width: pl.BlockSpec((1, 1, width), lambda n, g: (g, 0, 0))
    state = pl.BlockSpec((1, 1, SSM_STATE, SSD_GW), lambda n, g: (n, g, 0, 0))
    state_shape = jax.ShapeDtypeStruct((n_seq, SSM_GROUPS, SSM_STATE, SSD_GW), F32)
    cb2 = conv_b.reshape(1, SSM_CONV_DIM)
    return pl.pallas_call(
        _ssd_kernel,
        grid=(n_seq, SSM_GROUPS),
        in_specs=[
            cols(SSD_GW, 0), cols(SSD_GW, x0), cols(SSM_STATE, b0), cols(SSM_STATE, c0), cols(SSD_LANES, d0),
            wcols(SSM_CONV, SSD_GW, 0), wcols(1, SSD_GW, 0),
            wcols(SSM_CONV, SSM_STATE, SSM_D_INNER), wcols(1, SSM_STATE, SSM_D_INNER),
            wcols(SSM_CONV, SSM_STATE, SSM_D_INNER + SSM_GN), wcols(1, SSM_STATE, SSM_D_INNER + SSM_GN),
            pergroup(SSD_LANES), pergroup(SSD_LANES), pergroup(SSD_GW), pergroup(SSD_GW),
            pl.BlockSpec((2, SSD_LANES, SSD_GW), lambda n, g: (0, 0, 0)),
            state, state,
        ],
        out_specs=[
            pl.BlockSpec((1, length, SSD_GW), lambda n, g: (n // per_row, n % per_row, g)),
            state, state,
        ],
        out_shape=[jax.ShapeDtypeStruct((n_rows, seq, SSM_D_INNER), BF16), state_shape, state_shape],
        scratch_shapes=[
            pltpu.VMEM((length, SSD_GW), F32), pltpu.VMEM((length, SSM_STATE), F32),
            pltpu.VMEM((length, SSM_STATE), F32), pltpu.VMEM((length, SSD_LANES), F32),
            pltpu.VMEM((length, SSD_LANES), F32), pltpu.VMEM((length, SSD_LANES), F32),
            pltpu.VMEM((length, SSD_GW), F32),
            pltpu.VMEM((SSM_STATE, SSD_GW), F32),
        ],
        compiler_params=pltpu.CompilerParams(
            dimension_semantics=("arbitrary", "arbitrary"), vmem_limit_bytes=VMEM_LIMIT),
        name="ssd_core",
    )(zx, zx, zx, zx, zx, conv_w, cb2, conv_w, cb2, conv_w, cb2,
      dtb.reshape(SSM_GROUPS, 1, SSD_LANES), alog.reshape(SSM_GROUPS, 1, SSD_LANES),
      dskip.reshape(SSM_GROUPS, 1, SSD_GW), norm_g.reshape(SSM_GROUPS, 1, SSD_GW), expand, init_f, init_b)


def _regroup_heads(a, pad_value=0.0):
    lead = a.shape[:-2]
    a = a.reshape(lead + (2, SSM_GROUPS, SSD_HG))
    a = jnp.moveaxis(a, -3, -2).reshape(lead + (SSM_GROUPS, 2 * SSD_HG))
    pad = [(0, 0)] * (a.ndim - 1) + [(0, SSD_LANES - 2 * SSD_HG)]
    return jnp.pad(a, pad, constant_values=pad_value)


def _ssm_layer(s, m, w_in, conv_w, conv_b, dt_bias, a_log, d_skip, norm_g, w_out, ln_g, ln_b, ctx_out):
    rows, seq, _ = s.shape
    n_lat = rows - 1
    ctx_len = seq // n_lat
    n_main = SSM_D_INNER + SSM_CONV_DIM
    w_dt = _regroup_heads(w_in[:, n_main:].reshape(D_MODEL, 2, SSM_HEADS))
    w_all = jnp.concatenate([w_in[:, :n_main], w_dt.reshape(D_MODEL, SSM_GROUPS * SSD_LANES)], axis=1)
    zx = _mod_proj(s, m[:, 3:5], w_all.astype(BF16), jnp.zeros((SSD_IN_COLS,), F32), F32)
    dtb = _regroup_heads(dt_bias)
    alog = _regroup_heads(a_log)
    dskip = jnp.repeat(d_skip, SSM_HEAD_DIM).reshape(SSM_GROUPS, SSD_GW)
    lane = jnp.arange(SSD_LANES)[:, None]
    chan = jnp.arange(SSD_GW)[None, :] // SSM_HEAD_DIM
    expand = jnp.stack([lane == chan, lane == chan + SSD_HG]).astype(BF16)
    common = (conv_w, conv_b, dtb, alog, dskip, norm_g, expand)
    zero = jnp.zeros((n_lat, SSM_GROUPS, SSM_STATE, SSD_GW), F32)
    yc, s_f, s_b = _ssd(zx, ctx_len, n_lat, 1, *common, zero, zero)
    yl, _, _ = _ssd(zx, seq, 0, n_lat, *common, s_f, s_b)
    if ctx_out:
        y = jnp.concatenate([yl, yc], axis=0)
    else:
        y, s, m = yl, s[:n_lat], m[:n_lat]
    return _proj_norm(s, y, m[:, 5:6], w_out.astype(BF16), jnp.zeros((D_MODEL,), F32), ln_g, ln_b)


def kernel(x, c, ctx, c_ctx, ada_w, ada_b, ln_g, ln_b, ffn_w13, ffn_w2, attn_w_qkv, attn_w_o, attn_lambda, attn_subln_g, hy_w_in, hy_b_in, hy_conv_w, hy_conv_b, hy_filt_w_in, hy_filt_w_mid, hy_filt_b, hy_filt_freq, hy_filt_w_out, hy_bias, hy_w_out, hy_b_out, ssm_w_in, ssm_conv_w, ssm_conv_b, ssm_dt_bias, ssm_a_log, ssm_d, ssm_norm_g, ssm_w_out):
    c_rows = jnp.concatenate(
        [c, c_ctx[None], jnp.zeros((MOD_ROWS - N_ROWS, D_MODEL), F32)], axis=0)
    mod = _ada_all(c_rows, ada_w, ada_b)[:, :N_ROWS].reshape(DEPTH, N_ROWS, N_MOD, D_MODEL)
    w13_bf = (ffn_w13.reshape(DEPTH, 2, D_MODEL, 2, D_FF // FFN_TF, FFN_TF).swapaxes(3, 4)
              .reshape(DEPTH, 2, D_MODEL, 2 * D_FF).astype(BF16))
    w2_bf = ffn_w2.astype(BF16)

    s = jnp.concatenate([x, ctx.reshape(1, SEQ, D_MODEL)], axis=0)
    for i in range(DEPTH):
        kind, j = i % N_MIXERS, i // N_MIXERS
        ctx_out = i < DEPTH - 1
        m = mod[i]
        s = _ffn(s, m[:, 0:3], w13_bf[i, 0], w2_bf[i, 0], ln_g[i, 0], ln_b[i, 0])
        if kind == 0:
            s = _attention_layer(s, m, attn_w_qkv[j], attn_w_o[j], attn_lambda[j], attn_subln_g[j],
                                 ln_g[i, 1], ln_b[i, 1], i, ctx_out)
        elif kind == 1:
            s = _hyena_layer(s, m, hy_w_in[j], hy_b_in[j], hy_conv_w[j], hy_conv_b[j], hy_filt_w_in[j],
                             hy_filt_w_mid[j], hy_filt_b[j], hy_filt_freq[j], hy_filt_w_out[j], hy_bias[j],
                             hy_w_out[j], hy_b_out[j], ln_g[i, 1], ln_b[i, 1], ctx_out)
        else:
            s = _ssm_layer(s, m, ssm_w_in[j], ssm_conv_w[j], ssm_conv_b[j], ssm_dt_bias[j], ssm_a_log[j],
                           ssm_d[j], ssm_norm_g[j], ssm_w_out[j], ln_g[i, 1], ln_b[i, 1], ctx_out)
        m2 = m if ctx_out else m[:BATCH]
        s = _ffn(s, m2[:, 6:9], w13_bf[i, 1], w2_bf[i, 1], ln_g[i, 2], ln_b[i, 2])
    return s[:BATCH]
```

```python
import functools
import math

import jax
import jax.numpy as jnp
from jax import lax
from jax.experimental import pallas as pl
from jax.experimental.pallas import tpu as pltpu

F32 = jnp.float32
BF16 = jnp.bfloat16

D_MODEL = 1024
BATCH = 8
SEQ = 2048
DEPTH = 4
GRID_W = 64
CTX_LEN = 256
N_MIXERS = 3
DN_ALPHA = (2 * DEPTH) ** 0.25
LN_EPS = 1e-5
FFN_RES_W = 0.5
N_MOD = 9
D_FF = 2816
DA_HEADS = 8
DA_HEAD_DIM = D_MODEL // DA_HEADS // 2
DA_SCALE = DA_HEAD_DIM ** -0.5
Q_BLOCK = 128
ROPE_BASE = 10000.0
ROPE_AXIS_DIM = DA_HEAD_DIM // 2
ROPE_PAIRS = ROPE_AXIS_DIM // 2
HY_SHORT = 3
HY_EMB = 33
HY_BANDS = (HY_EMB - 1) // 2
HY_FILTER_W = 64
HY_TARGET = 1e-2
HY_MIN_DECAY = math.log(HY_TARGET) / 0.3
HY_MAX_DECAY = math.log(HY_TARGET) / 1.5
HY_SHIFT = 0.05
SSM_D_INNER = 2 * D_MODEL
SSM_HEAD_DIM = 64
SSM_HEADS = SSM_D_INNER // SSM_HEAD_DIM
SSM_GROUPS = 4
SSM_STATE = 128
SSM_CONV = 3
SSM_CHUNK = 128
SSM_GN = SSM_GROUPS * SSM_STATE
SSM_CONV_DIM = SSM_D_INNER + 2 * SSM_GN
SSM_IN_DIM = SSM_D_INNER + SSM_CONV_DIM + 2 * SSM_HEADS

N_ROWS = BATCH + 1
MOD_ROWS = 16
VMEM_LIMIT = 56 * 1024 * 1024


def _ada_kernel(c_ref, w_ref, b_ref, o_ref):
    c = c_ref[...]
    s = (c * jax.nn.sigmoid(c)).astype(BF16)
    o_ref[0] = jnp.dot(s, w_ref[0].astype(BF16), preferred_element_type=F32) + b_ref[0]


def _ada_all(c_rows, ada_w, ada_b):
    tn = 1024
    return pl.pallas_call(
        _ada_kernel,
        grid=(DEPTH, N_MOD * D_MODEL // tn),
        in_specs=[
            pl.BlockSpec((MOD_ROWS, D_MODEL), lambda i, j: (0, 0)),
            pl.BlockSpec((1, D_MODEL, tn), lambda i, j: (i, 0, j)),
            pl.BlockSpec((1, 1, tn), lambda i, j: (i, 0, j)),
        ],
        out_specs=pl.BlockSpec((1, MOD_ROWS, tn), lambda i, j: (i, 0, j)),
        out_shape=jax.ShapeDtypeStruct((DEPTH, MOD_ROWS, N_MOD * D_MODEL), F32),
        compiler_params=pltpu.CompilerParams(dimension_semantics=("arbitrary", "arbitrary")),
        name="ada_mod",
    )(c_rows, ada_w, ada_b.reshape(DEPTH, 1, N_MOD * D_MODEL))


FFN_TM = 1024
FFN_TF = 256


def _layer_norm_rows(z, g, b):
    mu = jnp.mean(z, axis=-1, keepdims=True)
    zc = z - mu
    var = jnp.mean(zc * zc, axis=-1, keepdims=True)
    return zc * lax.rsqrt(var + LN_EPS) * g + b


def _ffn_kernel(x_ref, mod_ref, w13_ref, w2_ref, lng_ref, lnb_ref, o_ref, g_ref):
    x = x_ref[0]
    m = mod_ref[0]
    h = (x * (1.0 + m[1:2]) + m[0:1]).astype(BF16)
    for j in range(D_FF // FFN_TF):
        cols = slice(j * FFN_TF, (j + 1) * FFN_TF)
        a = jnp.dot(h, w13_ref[:, cols], preferred_element_type=F32)
        u = jnp.dot(h, w13_ref[:, D_FF + j * FFN_TF:D_FF + (j + 1) * FFN_TF], preferred_element_type=F32)
        g_ref[:, cols] = (a * jax.nn.sigmoid(a) * u).astype(BF16)
    y = jnp.dot(g_ref[...], w2_ref[...], preferred_element_type=F32)
    z = DN_ALPHA * x + (FFN_RES_W * (1.0 + m[2:3])) * y
    o_ref[0] = _layer_norm_rows(z, lng_ref[...], lnb_ref[...])


def _ffn(xs, mod3, w13, w2, ln_g, ln_b):
    rows = xs.shape[0]
    const = dict(pipeline_mode=pl.Buffered(1))
    return pl.pallas_call(
        _ffn_kernel,
        grid=(rows, SEQ // FFN_TM),
        in_specs=[
            pl.BlockSpec((1, FFN_TM, D_MODEL), lambda r, t: (r, t, 0)),
            pl.BlockSpec((1, 3, D_MODEL), lambda r, t: (r, 0, 0)),
            pl.BlockSpec((D_MODEL, 2 * D_FF), lambda r, t: (0, 0), **const),
            pl.BlockSpec((D_FF, D_MODEL), lambda r, t: (0, 0), **const),
            pl.BlockSpec((1, D_MODEL), lambda r, t: (0, 0)),
            pl.BlockSpec((1, D_MODEL), lambda r, t: (0, 0)),
        ],
        out_specs=pl.BlockSpec((1, FFN_TM, D_MODEL), lambda r, t: (r, t, 0)),
        out_shape=jax.ShapeDtypeStruct(xs.shape, F32),
        scratch_shapes=[pltpu.VMEM((FFN_TM, D_FF), BF16)],
        compiler_params=pltpu.CompilerParams(
            dimension_semantics=("arbitrary", "arbitrary"), vmem_limit_bytes=VMEM_LIMIT),
        name="ffn",
    )(xs, mod3, w13, w2, ln_g.reshape(1, D_MODEL), ln_b.reshape(1, D_MODEL))


LOG2E = 1.4426950408889634
HEAD_W = 2 * DA_HEAD_DIM
QKV_TM = 512
ATTN_TQ = 1024
ATTN_SUB = 256


def _rope_tables(seq):
    pos = jnp.arange(seq, dtype=jnp.int32)
    row = (pos // GRID_W).astype(F32)
    col = (pos % GRID_W).astype(F32)
    inv = ROPE_BASE ** (-jnp.arange(ROPE_PAIRS, dtype=F32) / ROPE_PAIRS)
    ang_r = row[:, None] * inv
    ang_c = col[:, None] * inv
    ang = jnp.concatenate([ang_r, ang_r, ang_c, ang_c], axis=-1)
    sign = jnp.concatenate([-jnp.ones(ROPE_PAIRS), jnp.ones(ROPE_PAIRS)] * 2).astype(F32)
    cos = jnp.tile(jnp.cos(ang), (1, 2))
    sin = jnp.tile(jnp.sin(ang) * sign, (1, 2))
    return (jnp.stack([cos, jnp.ones_like(cos)]), jnp.stack([sin, jnp.zeros_like(sin)]))


def _qkv_kernel(x_ref, mod_ref, w_ref, cos_ref, sin_ref, q_ref, k_ref, v_ref):
    x = x_ref[0]
    m = mod_ref[0]
    h = (x * (1.0 + m[1:2]) + m[0:1]).astype(BF16)
    cos = cos_ref[0]
    sin = sin_ref[0]
    lane = lax.broadcasted_iota(jnp.int32, cos.shape, 1)
    first_half = (lane & (ROPE_AXIS_DIM - 1)) < ROPE_PAIRS

    def rope(blk):
        partner = jnp.where(first_half, pltpu.roll(blk, HEAD_W - ROPE_PAIRS, 1),
                            pltpu.roll(blk, ROPE_PAIRS, 1))
        return blk * cos + partner * sin

    q = jnp.dot(h, w_ref[:, 0:D_MODEL], preferred_element_type=F32)
    for j in range(DA_HEADS):
        sl = slice(j * HEAD_W, (j + 1) * HEAD_W)
        q_ref[0, :, sl] = (rope(q[:, sl]) * (DA_SCALE * LOG2E)).astype(BF16)
    k = jnp.dot(h, w_ref[:, D_MODEL:2 * D_MODEL], preferred_element_type=F32)
    for j in range(DA_HEADS):
        sl = slice(j * HEAD_W, (j + 1) * HEAD_W)
        k_ref[0, :, sl] = rope(k[:, sl]).astype(BF16)
    v_ref[0] = jnp.dot(h, w_ref[:, 2 * D_MODEL:], preferred_element_type=F32).astype(BF16)


def _qkv(xs, mod2, w_qkv, cos_tab, sin_tab):
    rows, seq, _ = xs.shape
    n_lat = rows - 1
    tok = pl.BlockSpec((1, QKV_TM, D_MODEL), lambda r, t: (r, t, 0))
    tab = pl.BlockSpec((1, QKV_TM, HEAD_W), lambda r, t: (r // n_lat, t, 0))
    out = jax.ShapeDtypeStruct(xs.shape, BF16)
    return pl.pallas_call(
        _qkv_kernel,
        grid=(rows, seq // QKV_TM),
        in_specs=[
            tok,
            pl.BlockSpec((1, 2, D_MODEL), lambda r, t: (r, 0, 0)),
            pl.BlockSpec((D_MODEL, 3 * D_MODEL), lambda r, t: (0, 0), pipeline_mode=pl.Buffered(1)),
            tab, tab,
        ],
        out_specs=[tok, tok, tok],
        out_shape=[out, out, out],
        compiler_params=pltpu.CompilerParams(
            dimension_semantics=("arbitrary", "arbitrary"), vmem_limit_bytes=VMEM_LIMIT),
        name="attn_qkv",
    )(xs, mod2, w_qkv, cos_tab, sin_tab)


def _attn_kernel(lam_ref, g_ref, q_ref, *refs, n_src, lam_init):
    k_refs, v_refs = refs[:n_src], refs[n_src:2 * n_src]
    o_ref, kcat, vcat = refs[2 * n_src:]

    @pl.when(pl.program_id(2) == 0)
    def _():
        off = 0
        for kr, vr in zip(k_refs, v_refs):
            n = kr.shape[1]
            kcat[off:off + n, :] = kr[0]
            vcat[off:off + n, :] = vr[0]
            off += n

    lam4 = lam_ref[...]
    lam_full = (jnp.exp(jnp.sum(lam4[0:1] * lam4[1:2], axis=-1, keepdims=True))
                - jnp.exp(jnp.sum(lam4[2:3] * lam4[3:4], axis=-1, keepdims=True)) + lam_init)
    k = kcat[...]
    v = vcat[...]
    nt = (((1,), (1,)), ((), ()))
    gain = g_ref[...] * (1.0 - lam_init)

    def softmax_av(qm):
        s = lax.dot_general(qm, k, nt, preferred_element_type=F32)
        e = jnp.exp2(s - jnp.max(s, axis=-1, keepdims=True))
        l = jnp.sum(e, axis=-1, keepdims=True)
        return jnp.dot(e.astype(BF16), v, preferred_element_type=F32) * (1.0 / l)

    tq = q_ref.shape[1]
    sub = min(ATTN_SUB, tq)
    for i in range(tq // sub):
        q = q_ref[0, i * sub:(i + 1) * sub, :]
        lane = lax.broadcasted_iota(jnp.int32, q.shape, 1)
        o = (softmax_av(jnp.where(lane < DA_HEAD_DIM, q, jnp.zeros_like(q)))
             - lam_full * softmax_av(jnp.where(lane >= DA_HEAD_DIM, q, jnp.zeros_like(q))))
        o = o * lax.rsqrt(jnp.mean(o * o, axis=-1, keepdims=True) + LN_EPS)
        o_ref[0, i * sub:(i + 1) * sub, :] = (o * gain).astype(BF16)


def _attend(lam, subln_g, q, k, v, lam_init, q_row0, n_q_rows, q_len, srcs):
    _, seq, _ = q.shape
    per_row = seq // q_len
    n_seq = n_q_rows * per_row
    tq = min(ATTN_TQ, q_len)
    nqt = q_len // tq
    n_keys = sum(s[2] for s in srcs)
    qspec = pl.BlockSpec((1, tq, HEAD_W),
                         lambda n, h, t: (q_row0 + n // per_row, (n % per_row) * nqt + t, h))
    kv_specs = [pl.BlockSpec((1, ln, HEAD_W), functools.partial(
        lambda n, h, t, rf, bf: (rf(n), bf(n), h), rf=rf, bf=bf)) for rf, bf, ln in srcs]
    ospec = pl.BlockSpec((1, tq, HEAD_W), lambda n, h, t: (n // per_row, (n % per_row) * nqt + t, h))
    return pl.pallas_call(
        functools.partial(_attn_kernel, n_src=len(srcs), lam_init=lam_init),
        grid=(n_seq, DA_HEADS, nqt),
        in_specs=[
            pl.BlockSpec((4, DA_HEAD_DIM), lambda n, h, t: (0, 0)),
            pl.BlockSpec((1, HEAD_W), lambda n, h, t: (0, 0)),
            qspec,
        ] + kv_specs + kv_specs,
        out_specs=ospec,
        out_shape=jax.ShapeDtypeStruct((n_q_rows, seq, D_MODEL), BF16),
        scratch_shapes=[pltpu.VMEM((n_keys, HEAD_W), BF16), pltpu.VMEM((n_keys, HEAD_W), BF16)],
        compiler_params=pltpu.CompilerParams(
            dimension_semantics=("arbitrary", "arbitrary", "arbitrary"), vmem_limit_bytes=VMEM_LIMIT),
        name="attn_core",
    )(lam, subln_g.reshape(1, HEAD_W), q, *([k] * len(srcs)), *([v] * len(srcs)))


PROJ_TM = 1024


def _proj_norm_kernel(x_ref, y_ref, gate_ref, w_ref, b_ref, lng_ref, lnb_ref, o_ref):
    y = jnp.dot(y_ref[0], w_ref[...], preferred_element_type=F32) + b_ref[...]
    z = DN_ALPHA * x_ref[0] + (1.0 + gate_ref[0]) * y
    o_ref[0] = _layer_norm_rows(z, lng_ref[...], lnb_ref[...])


def _proj_norm(xs, ys, gate, w, b, ln_g, ln_b):
    rows, seq, _ = xs.shape
    kdim = ys.shape[-1]
    vec = pl.BlockSpec((1, D_MODEL), lambda r, t: (0, 0))
    return pl.pallas_call(
        _proj_norm_kernel,
        grid=(rows, seq // PROJ_TM),
        in_specs=[
            pl.BlockSpec((1, PROJ_TM, D_MODEL), lambda r, t: (r, t, 0)),
            pl.BlockSpec((1, PROJ_TM, kdim), lambda r, t: (r, t, 0)),
            pl.BlockSpec((1, 1, D_MODEL), lambda r, t: (r, 0, 0)),
            pl.BlockSpec((kdim, D_MODEL), lambda r, t: (0, 0), pipeline_mode=pl.Buffered(1)),
            vec, vec, vec,
        ],
        out_specs=pl.BlockSpec((1, PROJ_TM, D_MODEL), lambda r, t: (r, t, 0)),
        out_shape=jax.ShapeDtypeStruct(xs.shape, F32),
        compiler_params=pltpu.CompilerParams(
            dimension_semantics=("arbitrary", "arbitrary"), vmem_limit_bytes=VMEM_LIMIT),
        name="proj_norm",
    )(xs, ys, gate, w, b.reshape(1, D_MODEL), ln_g.reshape(1, D_MODEL), ln_b.reshape(1, D_MODEL))


def _attention_layer(s, m, w_qkv, w_o, lam, subln_g, ln_g, ln_b, layer_idx, ctx_out):
    rows, seq, _ = s.shape
    n_lat = rows - 1
    ctx_len = seq // n_lat
    lam_init = 0.8 - 0.6 * math.exp(-0.3 * layer_idx)
    cos_tab, sin_tab = _rope_tables(seq)
    q, k, v = _qkv(s, m[:, 3:5], w_qkv.astype(BF16), cos_tab, sin_tab)
    ctx_src = (lambda n: n_lat, lambda n: n, ctx_len)
    o = _attend(lam, subln_g, q, k, v, lam_init, 0, n_lat, seq,
                [ctx_src, (lambda n: n, lambda n: 0, seq)])
    if ctx_out:
        oc = _attend(lam, subln_g, q, k, v, lam_init, n_lat, 1, ctx_len, [ctx_src])
        o = jnp.concatenate([o, oc], axis=0)
    else:
        s, m = s[:n_lat], m[:n_lat]
    return _proj_norm(s, o, m[:, 5:6], w_o.astype(BF16), jnp.zeros((D_MODEL,), F32), ln_g, ln_b)


MODPROJ_TM = 512


def _mod_proj_kernel(x_ref, mod_ref, w_ref, b_ref, o_ref):
    m = mod_ref[0]
    h = (x_ref[0] * (1.0 + m[1:2]) + m[0:1]).astype(BF16)
    o_ref[0] = (jnp.dot(h, w_ref[...], preferred_element_type=F32) + b_ref[...]).astype(o_ref.dtype)


def _mod_proj(xs, mod2, w, b, out_dtype):
    rows, seq, _ = xs.shape
    n = w.shape[1]
    return pl.pallas_call(
        _mod_proj_kernel,
        grid=(rows, seq // MODPROJ_TM),
        in_specs=[
            pl.BlockSpec((1, MODPROJ_TM, D_MODEL), lambda r, t: (r, t, 0)),
            pl.BlockSpec((1, 2, D_MODEL), lambda r, t: (r, 0, 0)),
            pl.BlockSpec((D_MODEL, n), lambda r, t: (0, 0), pipeline_mode=pl.Buffered(1)),
            pl.BlockSpec((1, n), lambda r, t: (0, 0)),
        ],
        out_specs=pl.BlockSpec((1, MODPROJ_TM, n), lambda r, t: (r, t, 0)),
        out_shape=jax.ShapeDtypeStruct((rows, seq, n), out_dtype),
        compiler_params=pltpu.CompilerParams(
            dimension_semantics=("arbitrary", "arbitrary"), vmem_limit_bytes=VMEM_LIMIT),
        name="mod_proj",
    )(xs, mod2, w, b.reshape(1, n))


HY_PAD = 128
HY_TD = 256
HY_BANDS_SPLIT = 8
HY_MIN_BAND = 256


def _dft_tables(length):
    n = 2 * length
    idx = jnp.arange(length, dtype=jnp.int32)
    ang = ((idx[:, None] * idx[None, :]) % n).astype(F32) * (2.0 * math.pi / n)
    return jnp.cos(ang).astype(BF16), jnp.sin(ang).astype(BF16)


def _hyena_tables(length):
    t = jnp.linspace(0.0, 1.0, length, dtype=F32)[:, None]
    w = 2.0 * math.pi * jnp.arange(length, dtype=F32)[:, None] / length
    f = jnp.linspace(1e-4, HY_BANDS - 1, HY_BANDS, dtype=F32)
    z = jnp.concatenate([t, jnp.cos(f * w), -jnp.sin(f * w)], axis=-1)
    deltas = jnp.abs(jnp.linspace(HY_MIN_DECAY, HY_MAX_DECAY, D_MODEL, dtype=F32))
    window = jnp.exp(-t * deltas) + HY_SHIFT
    return jnp.pad(z, ((0, 0), (0, HY_PAD - HY_EMB))), window


def _split_dot(mat_ref, x):
    hi = x.astype(BF16)
    lo = (x - hi.astype(F32)).astype(BF16)
    mat = mat_ref[...]
    return (jnp.dot(mat, hi, preferred_element_type=F32) + jnp.dot(mat, lo, preferred_element_type=F32))


def _hy_filter_kernel(z_ref, win_ref, wmid_ref, b_ref, freq_ref, woutf_ref, woutb_ref, window_ref,
                      c_ref, s_ref, kc_ref, ks_ref, kn_ref):
    exact = dict(precision=lax.Precision.HIGHEST, preferred_element_type=F32)
    freq = freq_ref[...]
    b = b_ref[...]
    h = jnp.sin(freq * (jnp.dot(z_ref[...], win_ref[...], **exact) + b[0:1]))
    h = jnp.sin(freq * (jnp.dot(h, wmid_ref[0], **exact) + b[1:2]))
    h = jnp.sin(freq * (jnp.dot(h, wmid_ref[1], **exact) + b[2:3]))
    window = window_ref[...]
    row = lax.broadcasted_iota(jnp.int32, window.shape, 0)
    hf = jnp.dot(h, woutf_ref[...], **exact) * window
    hb = jnp.where(row == 0, 0.0, jnp.dot(h, woutb_ref[...], **exact) * window)
    n = 2 * window.shape[0]
    wgt = jnp.where(row == 0, 1.0 / n, 2.0 / n)
    ssum = hf + hb
    kc_ref[...] = wgt * _split_dot(c_ref, ssum)
    ks_ref[...] = wgt * _split_dot(s_ref, hf - hb)
    sign = jnp.where((row & 1) == 0, 1.0, -1.0)
    kn_ref[...] = jnp.sum(ssum * sign, axis=0, keepdims=True) * (1.0 / n)


def _hy_filter(length, f_w_in, f_w_mid, f_b, f_freq, f_w_out, cmat, smat):
    z, window = _hyena_tables(length)
    pad_w = HY_PAD - HY_FILTER_W
    w_in = jnp.pad(f_w_in, ((0, HY_PAD - HY_EMB), (0, pad_w)))
    w_mid = jnp.pad(f_w_mid, ((0, 0), (0, pad_w), (0, pad_w)))
    b = jnp.pad(f_b, ((0, 8 - f_b.shape[0]), (0, pad_w)))
    freq = jnp.pad(f_freq, (0, pad_w)).reshape(1, HY_PAD)
    w_out = jnp.pad(f_w_out, ((0, pad_w), (0, 0)))
    full = lambda shape: pl.BlockSpec(shape, lambda c: (0,) * len(shape))
    ncb = D_MODEL // HY_TD
    col = pl.BlockSpec((length, HY_TD), lambda c: (0, c))
    mat = pl.BlockSpec((length, length), lambda c: (0, 0), pipeline_mode=pl.Buffered(1))
    return pl.pallas_call(
        _hy_filter_kernel,
        grid=(ncb,),
        in_specs=[
            full((length, HY_PAD)), full((HY_PAD, HY_PAD)), full((2, HY_PAD, HY_PAD)), full((8, HY_PAD)),
            full((1, HY_PAD)),
            pl.BlockSpec((HY_PAD, HY_TD), lambda c: (0, c)),
            pl.BlockSpec((HY_PAD, HY_TD), lambda c: (0, ncb + c)),
            col, mat, mat,
        ],
        out_specs=[col, col, pl.BlockSpec((1, HY_TD), lambda c: (0, c))],
        out_shape=[jax.ShapeDtypeStruct((length, D_MODEL), F32), jax.ShapeDtypeStruct((length, D_MODEL), F32),
                   jax.ShapeDtypeStruct((1, D_MODEL), F32)],
        compiler_params=pltpu.CompilerParams(
            dimension_semantics=("arbitrary",), vmem_limit_bytes=VMEM_LIMIT),
        name="hy_filter",
    )(z, w_in, w_mid, b, freq, w_out, w_out, window, cmat, smat)


def _hy_conv_kernel(x0_ref, x1_ref, v_ref, cw_ref, cb_ref, kc_ref, ks_ref, kn_ref, bias_ref,
                    c_ref, s_ref, o_ref):
    length, td = kc_ref.shape
    row = lax.broadcasted_iota(jnp.int32, (length, td), 0)
    cw = cw_ref[...]
    cb = cb_ref[...]

    def short_conv(u, p):
        prev = jnp.where(row == 0, 0.0, pltpu.roll(u, 1, 0))
        nxt = jnp.where(row == length - 1, 0.0, pltpu.roll(u, length - 1, 0))
        return cb[p:p + 1] + prev * cw[0, p:p + 1] + u * cw[1, p:p + 1] + nxt * cw[2, p:p + 1]

    vv = short_conv(v_ref[0], 2) * short_conv(x1_ref[0], 1)
    vvb = vv.astype(BF16)
    y = None
    band = max(length // HY_BANDS_SPLIT, HY_MIN_BAND)
    for f0 in range(0, length, band):
        fr = slice(f0, f0 + band)
        vc = jnp.dot(c_ref[fr, :], vvb, preferred_element_type=F32)
        vs = jnp.dot(s_ref[fr, :], vvb, preferred_element_type=F32)
        kc = kc_ref[fr, :]
        ks = ks_ref[fr, :]
        yc = (vc * kc - vs * ks).astype(BF16)
        ys = (vc * ks + vs * kc).astype(BF16)
        part = (jnp.dot(c_ref[:, fr], yc, preferred_element_type=F32)
                + jnp.dot(s_ref[:, fr], ys, preferred_element_type=F32))
        y = part if y is None else y + part
    sign = jnp.where((row & 1) == 0, 1.0, -1.0)
    nyq = jnp.sum(vv * sign, axis=0, keepdims=True) * kn_ref[...]
    y = y + sign * nyq + vv * bias_ref[...]
    o_ref[0] = (y * short_conv(x0_ref[0], 0)).astype(BF16)


def _hy_conv(u, length, row0, n_rows, conv_w, conv_b, kc, ks, kn, bias, cmat, smat):
    _, seq, _ = u.shape
    per_row = seq // length
    ncb = D_MODEL // HY_TD
    part = lambda p: pl.BlockSpec(
        (1, length, HY_TD), lambda c, n: (row0 + n // per_row, n % per_row, p * ncb + c))
    col = pl.BlockSpec((length, HY_TD), lambda c, n: (0, c), pipeline_mode=pl.Buffered(1))
    vec = pl.BlockSpec((1, HY_TD), lambda c, n: (0, c))
    mat = pl.BlockSpec((length, length), lambda c, n: (0, 0), pipeline_mode=pl.Buffered(1))
    return pl.pallas_call(
        _hy_conv_kernel,
        grid=(ncb, n_rows * per_row),
        in_specs=[
            part(0), part(1), part(2),
            pl.BlockSpec((HY_SHORT, 3, HY_TD), lambda c, n: (0, 0, c)),
            pl.BlockSpec((3, HY_TD), lambda c, n: (0, c)),
            col, col, vec, vec, mat, mat,
        ],
        out_specs=pl.BlockSpec((1, length, HY_TD), lambda c, n: (n // per_row, n % per_row, c)),
        out_shape=jax.ShapeDtypeStruct((n_rows, seq, D_MODEL), BF16),
        compiler_params=pltpu.CompilerParams(
            dimension_semantics=("arbitrary", "arbitrary"), vmem_limit_bytes=VMEM_LIMIT),
        name="hy_conv",
    )(u, u, u, conv_w.reshape(HY_SHORT, 3, D_MODEL), conv_b.reshape(3, D_MODEL), kc, ks, kn,
      bias.reshape(1, D_MODEL), cmat, smat)


def _hyena_layer(s, m, w_in, b_in, conv_w, conv_b, f_w_in, f_w_mid, f_b, f_freq, f_w_out, bias,
                 w_out, b_out, ln_g, ln_b, ctx_out):
    rows, seq, _ = s.shape
    n_lat = rows - 1
    if not ctx_out:
        s, m = s[:n_lat], m[:n_lat]
    u = _mod_proj(s, m[:, 3:5], w_in.astype(BF16), b_in, F32)
    filt = (f_w_in, f_w_mid, f_b, f_freq, f_w_out)

    def mix(length, row0, n_rows):
        cmat, smat = _dft_tables(length)
        kc, ks, kn = _hy_filter(length, *filt, cmat, smat)
        return _hy_conv(u, length, row0, n_rows, conv_w, conv_b, kc, ks, kn, bias, cmat, smat)

    y = mix(seq, 0, n_lat)
    if ctx_out:
        y = jnp.concatenate([y, mix(seq // n_lat, n_lat, 1)], axis=0)
    return _proj_norm(s, y, m[:, 5:6], w_out.astype(BF16), b_out, ln_g, ln_b)


SSD_HG = SSM_HEADS // SSM_GROUPS
SSD_GW = SSD_HG * SSM_HEAD_DIM
SSD_LANES = 128
SSD_IN_COLS = SSM_D_INNER + SSM_CONV_DIM + SSM_GROUPS * SSD_LANES


def _split3(x):
    hi = x.astype(BF16)
    r = x - hi.astype(F32)
    mid = r.astype(BF16)
    return hi, mid, (r - mid.astype(F32)).astype(BF16)


def _exact_lhs_dot(sel, x):
    return sum(jnp.dot(sel, t, preferred_element_type=F32) for t in _split3(x))


def _rhs_dot2(x, sel):
    return sum(jnp.dot(t, sel, preferred_element_type=F32) for t in _split3(x)[:2])


def _ssd_kernel(z_ref, x_ref, b_ref, c_ref, dt_ref, cwx_ref, cbx_ref, cwb_ref, cbb_ref, cwc_ref, cbc_ref,
                dtb_ref, alog_ref, dskip_ref, ng_ref, expand_ref, initf_ref, initb_ref,
                o_ref, finf_ref, finb_ref, xs_ref, bs_ref, cs_ref, a_ref, dts_ref, cum_ref, y_ref, st_ref):
    length = x_ref.shape[1]
    q = SSM_CHUNK
    nc = length // q

    def conv_silu(u, cw, cb):
        row = lax.broadcasted_iota(jnp.int32, u.shape, 0)
        prev = jnp.where(row == 0, 0.0, pltpu.roll(u, 1, 0))
        nxt = jnp.where(row == length - 1, 0.0, pltpu.roll(u, length - 1, 0))
        v = cb + prev * cw[0:1] + u * cw[1:2] + nxt * cw[2:3]
        return v * jax.nn.sigmoid(v)

    xs_ref[...] = conv_silu(x_ref[0], cwx_ref[...], cbx_ref[...])
    bs_ref[...] = conv_silu(b_ref[0], cwb_ref[...], cbb_ref[...])
    cs_ref[...] = conv_silu(c_ref[0], cwc_ref[...], cbc_ref[...])
    pre = dt_ref[0] + dtb_ref[0]
    dt = jnp.maximum(pre, 0.0) + jnp.log(1.0 + jnp.exp(-jnp.abs(pre)))
    dts_ref[...] = dt
    a_ref[...] = dt * (-jnp.exp(alog_ref[0]))

    ri = lax.broadcasted_iota(jnp.int32, (q, q), 0)
    ci = lax.broadcasted_iota(jnp.int32, (q, q), 1)
    nt = (((1,), (1,)), ((), ()))
    tri = jnp.where(ri >= ci, 1.0, 0.0).astype(BF16)
    for c in range(nc):
        cum_ref[c * q:(c + 1) * q, :] = _exact_lhs_dot(tri, a_ref[c * q:(c + 1) * q, :])

    for d, (init_ref, fin_ref) in enumerate(((initf_ref, finf_ref), (initb_ref, finb_ref))):
        fwd = d == 0
        mask = (ri >= ci) if fwd else (ri <= ci)
        end = q - 1 if fwd else 0
        expand = expand_ref[d]
        st_ref[...] = init_ref[0, 0]

        def body(i, carry, fwd=fwd, mask=mask, end=end, expand=expand, d=d):
            c = i if fwd else nc - 1 - i
            rows = pl.ds(pl.multiple_of(c * q, q), q)
            cum = cum_ref[rows, :]
            if not fwd:
                cum = cum[q - 1:q, :] - cum + a_ref[rows, :]
            cum_t = cum.T
            dtc = dts_ref[rows, :]
            dt_t = dtc.T
            ecum = jnp.exp(cum)
            w_e = _rhs_dot2(dtc * jnp.exp(cum[end:end + 1, :] - cum), expand)
            xs = xs_ref[rows, :]
            xsb = xs.astype(BF16)
            bc = bs_ref[rows, :]
            ccf = cs_ref[rows, :]
            cb = lax.dot_general(ccf.astype(BF16), bc.astype(BF16), nt, preferred_element_type=F32)
            st = st_ref[...]
            stb = st.astype(BF16)
            ys = []
            for p in range(SSD_HG // 2):
                rhs = xsb[:, p * SSD_LANES:(p + 1) * SSD_LANES]
                pair = []
                for j in (2 * p, 2 * p + 1):
                    lane = d * SSD_HG + j
                    seg = cum[:, lane:lane + 1] - cum_t[lane:lane + 1, :]
                    lmat = (jnp.where(mask, jnp.exp(jnp.minimum(seg, 0.0)), 0.0)
                            * (cb * dt_t[lane:lane + 1, :]))
                    pair.append(jnp.dot(lmat.astype(BF16), rhs, preferred_element_type=F32))
                ys.append(jnp.where(ci < SSM_HEAD_DIM, pair[0], pair[1]))
            e_e = _rhs_dot2(ecum, expand)
            y = (jnp.concatenate(ys, axis=1)
                 + jnp.dot(ccf.astype(BF16), stb, preferred_element_type=F32) * e_e)
            if fwd:
                y_ref[rows, :] = y
            else:
                y_ref[rows, :] += y
            upd = jnp.dot(bc.T.astype(BF16), (xs * w_e).astype(BF16), preferred_element_type=F32)
            st_ref[...] = st * e_e[end:end + 1, :] + upd
            return carry

        lax.fori_loop(0, nc, body, 0, unroll=2)
        fin_ref[0, 0] = st_ref[...]

    z = z_ref[0]
    yg = (y_ref[...] + xs_ref[...] * dskip_ref[0]) * (z * jax.nn.sigmoid(z))
    yn = yg * lax.rsqrt(jnp.mean(yg * yg, axis=-1, keepdims=True) + LN_EPS)
    o_ref[0] = (yn * ng_ref[0]).astype(BF16)


def _ssd(zx, length, row0, n_rows, conv_w, conv_b, dtb, alog, dskip, norm_g, expand, init_f, init_b):
    _, seq, _ = zx.shape
    per_row = seq // length
    n_seq = n_rows * per_row
    q = SSM_CHUNK

    def cols(width, first):
        return pl.BlockSpec((1, length, width),
                            lambda n, g: (row0 + n // per_row, n % per_row, first // width + g))

    def wcols(nrow, width, first):
        return pl.BlockSpec((nrow, width), lambda n, g: (0, first // width + g))

    x0 = SSM_D_INNER
    b0 = 2 * SSM_D_INNER
    c0 = b0 + SSM_GN
    d0 = c0 + SSM_GN
    pergroup = lambda width: pl.BlockSpec((1, 1, width), lambda n, g: (g, 0, 0))
    state = pl.BlockSpec((1, 1, SSM_STATE, SSD_GW), lambda n, g: (n, g, 0, 0))
    state_shape = jax.ShapeDtypeStruct((n_seq, SSM_GROUPS, SSM_STATE, SSD_GW), F32)
    cb2 = conv_b.reshape(1, SSM_CONV_DIM)
    return pl.pallas_call(
        _ssd_kernel,
        grid=(n_seq, SSM_GROUPS),
        in_specs=[
            cols(SSD_GW, 0), cols(SSD_GW, x0), cols(SSM_STATE, b0), cols(SSM_STATE, c0), cols(SSD_LANES, d0),
            wcols(SSM_CONV, SSD_GW, 0), wcols(1, SSD_GW, 0),
            wcols(SSM_CONV, SSM_STATE, SSM_D_INNER), wcols(1, SSM_STATE, SSM_D_INNER),
            wcols(SSM_CONV, SSM_STATE, SSM_D_INNER + SSM_GN), wcols(1, SSM_STATE, SSM_D_INNER + SSM_GN),
            pergroup(SSD_LANES), pergroup(SSD_LANES), pergroup(SSD_GW), pergroup(SSD_GW),
            pl.BlockSpec((2, SSD_LANES, SSD_GW), lambda n, g: (0, 0, 0)),
            state, state,
        ],
        out_specs=[
            pl.BlockSpec((1, length, SSD_GW), lambda n, g: (n // per_row, n % per_row, g)),
            state, state,
        ],
        out_shape=[jax.ShapeDtypeStruct((n_rows, seq, SSM_D_INNER), BF16), state_shape, state_shape],
        scratch_shapes=[
            pltpu.VMEM((length, SSD_GW), F32), pltpu.VMEM((length, SSM_STATE), F32),
            pltpu.VMEM((length, SSM_STATE), F32), pltpu.VMEM((length, SSD_LANES), F32),
            pltpu.VMEM((length, SSD_LANES), F32), pltpu.VMEM((length, SSD_LANES), F32),
            pltpu.VMEM((length, SSD_GW), F32),
            pltpu.VMEM((SSM_STATE, SSD_GW), F32),
        ],
        compiler_params=pltpu.CompilerParams(
            dimension_semantics=("arbitrary", "arbitrary"), vmem_limit_bytes=VMEM_LIMIT),
        name="ssd_core",
    )(zx, zx, zx, zx, zx, conv_w, cb2, conv_w, cb2, conv_w, cb2,
      dtb.reshape(SSM_GROUPS, 1, SSD_LANES), alog.reshape(SSM_GROUPS, 1, SSD_LANES),
      dskip.reshape(SSM_GROUPS, 1, SSD_GW), norm_g.reshape(SSM_GROUPS, 1, SSD_GW), expand, init_f, init_b)


def _regroup_heads(a, pad_value=0.0):
    lead = a.shape[:-2]
    a = a.reshape(lead + (2, SSM_GROUPS, SSD_HG))
    a = jnp.moveaxis(a, -3, -2).reshape(lead + (SSM_GROUPS, 2 * SSD_HG))
    pad = [(0, 0)] * (a.ndim - 1) + [(0, SSD_LANES - 2 * SSD_HG)]
    return jnp.pad(a, pad, constant_values=pad_value)


def _ssm_layer(s, m, w_in, conv_w, conv_b, dt_bias, a_log, d_skip, norm_g, w_out, ln_g, ln_b, ctx_out):
    rows, seq, _ = s.shape
    n_lat = rows - 1
    ctx_len = seq // n_lat
    n_main = SSM_D_INNER + SSM_CONV_DIM
    w_dt = _regroup_heads(w_in[:, n_main:].reshape(D_MODEL, 2, SSM_HEADS))
    w_all = jnp.concatenate([w_in[:, :n_main], w_dt.reshape(D_MODEL, SSM_GROUPS * SSD_LANES)], axis=1)
    zx = _mod_proj(s, m[:, 3:5], w_all.astype(BF16), jnp.zeros((SSD_IN_COLS,), F32), F32)
    dtb = _regroup_heads(dt_bias)
    alog = _regroup_heads(a_log)
    dskip = jnp.repeat(d_skip, SSM_HEAD_DIM).reshape(SSM_GROUPS, SSD_GW)
    lane = jnp.arange(SSD_LANES)[:, None]
    chan = jnp.arange(SSD_GW)[None, :] // SSM_HEAD_DIM
    expand = jnp.stack([lane == chan, lane == chan + SSD_HG]).astype(BF16)
    common = (conv_w, conv_b, dtb, alog, dskip, norm_g, expand)
    zero = jnp.zeros((n_lat, SSM_GROUPS, SSM_STATE, SSD_GW), F32)
    yc, s_f, s_b = _ssd(zx, ctx_len, n_lat, 1, *common, zero, zero)
    yl, _, _ = _ssd(zx, seq, 0, n_lat, *common, s_f, s_b)
    if ctx_out:
        y = jnp.concatenate([yl, yc], axis=0)
    else:
        y, s, m = yl, s[:n_lat], m[:n_lat]
    return _proj_norm(s, y, m[:, 5:6], w_out.astype(BF16), jnp.zeros((D_MODEL,), F32), ln_g, ln_b)


def kernel(x, c, ctx, c_ctx, ada_w, ada_b, ln_g, ln_b, ffn_w13, ffn_w2, attn_w_qkv, attn_w_o, attn_lambda, attn_subln_g, hy_w_in, hy_b_in, hy_conv_w, hy_conv_b, hy_filt_w_in, hy_filt_w_mid, hy_filt_b, hy_filt_freq, hy_filt_w_out, hy_bias, hy_w_out, hy_b_out, ssm_w_in, ssm_conv_w, ssm_conv_b, ssm_dt_bias, ssm_a_log, ssm_d, ssm_norm_g, ssm_w_out):
    c_rows = jnp.concatenate(
        [c, c_ctx[None], jnp.zeros((MOD_ROWS - N_ROWS, D_MODEL), F32)], axis=0)
    mod = _ada_all(c_rows, ada_w, ada_b)[:, :N_ROWS].reshape(DEPTH, N_ROWS, N_MOD, D_MODEL)
    w13_bf = ffn_w13.astype(BF16)
    w2_bf = ffn_w2.astype(BF16)

    s = jnp.concatenate([x, ctx.reshape(1, SEQ, D_MODEL)], axis=0)
    for i in range(DEPTH):
        kind, j = i % N_MIXERS, i // N_MIXERS
        ctx_out = i < DEPTH - 1
        m = mod[i]
        s = _ffn(s, m[:, 0:3], w13_bf[i, 0], w2_bf[i, 0], ln_g[i, 0], ln_b[i, 0])
        if kind == 0:
            s = _attention_layer(s, m, attn_w_qkv[j], attn_w_o[j], attn_lambda[j], attn_subln_g[j],
                                 ln_g[i, 1], ln_b[i, 1], i, ctx_out)
        elif kind == 1:
            s = _hyena_layer(s, m, hy_w_in[j], hy_b_in[j], hy_conv_w[j], hy_conv_b[j], hy_filt_w_in[j],
                             hy_filt_w_mid[j], hy_filt_b[j], hy_filt_freq[j], hy_filt_w_out[j], hy_bias[j],
                             hy_w_out[j], hy_b_out[j], ln_g[i, 1], ln_b[i, 1], ctx_out)
        else:
            s = _ssm_layer(s, m, ssm_w_in[j], ssm_conv_w[j], ssm_conv_b[j], ssm_dt_bias[j], ssm_a_log[j],
                           ssm_d[j], ssm_norm_g[j], ssm_w_out[j], ln_g[i, 1], ln_b[i, 1], ctx_out)
        m2 = m if ctx_out else m[:BATCH]
        s = _ffn(s, m2[:, 6:9], w13_bf[i, 1], w2_bf[i, 1], ln_g[i, 2], ln_b[i, 2])
    return s[:BATCH]
```
